```python
import jax, jax.numpy as jnp
from jax import lax
import numpy as np

D_MODEL = 1024
BATCH = 16
SEQ = 2048
DEPTH = 4

N_CONV_LAYERS = DEPTH // 2
N_MLA_LAYERS = DEPTH - N_CONV_LAYERS
CONV_WIDTH = 3
V_HEAD_DIM = 128
N_HEADS = D_MODEL // V_HEAD_DIM
QK_NOPE_DIM = 128
QK_ROPE_DIM = 64
KV_LORA_RANK = D_MODEL // 4
Q_LORA_RANK = 3 * D_MODEL // 8
D_FF = -(-8 * D_MODEL // (3 * 256)) * 256
ROPE_THETA = 10000.0
RMS_EPS = 1e-6
Q_BLOCK = 128

kernel_name = "yoco_shortconv_mla_hybrid"


def rms_norm(x, g):
    xf = x.astype(jnp.float32)
    y = xf * lax.rsqrt(jnp.mean(xf * xf, axis=-1, keepdims=True) + RMS_EPS)
    return (y * g.astype(jnp.float32)).astype(x.dtype)


def rope_tables(positions):
    inv_freq = ROPE_THETA ** (-jnp.arange(0, QK_ROPE_DIM, 2, dtype=jnp.float32) / QK_ROPE_DIM)
    ang = positions.astype(jnp.float32)[..., None] * inv_freq
    return jnp.cos(ang), jnp.sin(ang)


def apply_rope(x, cos, sin):
    cos = cos.astype(x.dtype)
    sin = sin.astype(x.dtype)
    x1, x2 = jnp.split(x, 2, axis=-1)
    return jnp.concatenate([x1 * cos - x2 * sin, x1 * sin + x2 * cos], axis=-1)


def short_conv_mixer(h, w_in, conv_w, w_out):
    s = h.shape[1]
    u = h @ w_in
    gate_b, gate_c, val = jnp.split(u, 3, axis=-1)
    z = gate_c * val
    zp = jnp.pad(z, ((0, 0), (CONV_WIDTH - 1, 0), (0, 0)))
    conv = sum(conv_w[k] * zp[:, k:k + s] for k in range(CONV_WIDTH))
    return (gate_b * conv) @ w_out


def swiglu_ffn(h, w13, w2):
    g, u = jnp.split(h @ w13, 2, axis=-1)
    return (jax.nn.silu(g) * u) @ w2


def shared_mla_kv(h, kv_norm_g, w_dkv, kv_latent_norm_g, w_ukv, cos, sin):
    b, s, _ = h.shape
    hn = rms_norm(h, kv_norm_g)
    ckv = hn @ w_dkv
    c_lat = rms_norm(ckv[..., :KV_LORA_RANK], kv_latent_norm_g)
    k_pe = apply_rope(ckv[..., KV_LORA_RANK:], cos, sin)
    kv = (c_lat @ w_ukv).reshape(b, s, N_HEADS, QK_NOPE_DIM + V_HEAD_DIM)
    k_nope, v = kv[..., :QK_NOPE_DIM], kv[..., QK_NOPE_DIM:]
    return k_nope, k_pe, v


def causal_block_attention(q_nope, q_pe, k_nope, k_pe, v):
    b, s = q_nope.shape[:2]
    nb = s // Q_BLOCK
    qn = q_nope.reshape(b, nb, Q_BLOCK, N_HEADS, QK_NOPE_DIM).transpose(1, 0, 2, 3, 4)
    qr = q_pe.reshape(b, nb, Q_BLOCK, N_HEADS, QK_ROPE_DIM).transpose(1, 0, 2, 3, 4)
    starts = jnp.arange(nb, dtype=jnp.int32) * Q_BLOCK
    k_idx = jnp.arange(s, dtype=jnp.int32)
    scale = (QK_NOPE_DIM + QK_ROPE_DIM) ** -0.5
    neg = jnp.finfo(jnp.float32).min

    def one_block(args):
        qn_b, qr_b, start = args
        sc = (jnp.einsum('bqhd,bkhd->bhqk', qn_b, k_nope)
              + jnp.einsum('bqhr,bkr->bhqk', qr_b, k_pe))
        sc = sc.astype(jnp.float32) * scale
        q_idx = start + jnp.arange(Q_BLOCK, dtype=jnp.int32)
        mask = k_idx[None, :] <= q_idx[:, None]
        p = jax.nn.softmax(jnp.where(mask, sc, neg), axis=-1).astype(v.dtype)
        return jnp.einsum('bhqk,bkhd->bqhd', p, v)

    out = lax.map(one_block, (qn, qr, starts))
    return out.transpose(1, 0, 2, 3, 4).reshape(b, s, N_HEADS, V_HEAD_DIM)


def mla_mixer(h, w_dq, q_norm_g, w_uq, w_o, k_nope, k_pe, v, cos_h, sin_h):
    b, s, _ = h.shape
    cq = rms_norm(h @ w_dq, q_norm_g)
    q = (cq @ w_uq).reshape(b, s, N_HEADS, QK_NOPE_DIM + QK_ROPE_DIM)
    q_nope = q[..., :QK_NOPE_DIM]
    q_pe = apply_rope(q[..., QK_NOPE_DIM:], cos_h, sin_h)
    attn = causal_block_attention(q_nope, q_pe, k_nope, k_pe, v)
    return attn.reshape(b, s, N_HEADS * V_HEAD_DIM) @ w_o


def setup_inputs(seed: int = 0) -> dict:
    key = jax.random.key(seed)
    ks = iter(jax.random.split(key, 32))
    f32 = jnp.float32
    res_scale = (2 * DEPTH) ** -0.5

    def w(shape, fan_in, extra=1.0):
        return jax.random.normal(next(ks), shape, f32) * (fan_in ** -0.5) * extra

    def gain(shape):
        return 1.0 + 0.02 * jax.random.normal(next(ks), shape, f32)

    na, nm = N_CONV_LAYERS, N_MLA_LAYERS
    x = jax.random.normal(next(ks), (BATCH, SEQ, D_MODEL), f32)
    positions = jnp.broadcast_to(jnp.arange(SEQ, dtype=jnp.int32), (BATCH, SEQ))
    return {
        "x": x,
        "positions": positions,
        "conv_norm_g": gain((na, D_MODEL)),
        "conv_w_in": w((na, D_MODEL, 3 * D_MODEL), D_MODEL),
        "conv_w": w((na, CONV_WIDTH, D_MODEL), CONV_WIDTH),
        "conv_w_out": w((na, D_MODEL, D_MODEL), D_MODEL, res_scale),
        "conv_ffn_norm_g": gain((na, D_MODEL)),
        "conv_ffn_w13": w((na, D_MODEL, 2 * D_FF), D_MODEL),
        "conv_ffn_w2": w((na, D_FF, D_MODEL), D_FF, res_scale),
        "kv_norm_g": gain((D_MODEL,)),
        "w_dkv": w((D_MODEL, KV_LORA_RANK + QK_ROPE_DIM), D_MODEL),
        "kv_latent_norm_g": gain((KV_LORA_RANK,)),
        "w_ukv": w((KV_LORA_RANK, N_HEADS * (QK_NOPE_DIM + V_HEAD_DIM)), KV_LORA_RANK),
        "mla_norm_g": gain((nm, D_MODEL)),
        "mla_w_dq": w((nm, D_MODEL, Q_LORA_RANK), D_MODEL),
        "mla_q_norm_g": gain((nm, Q_LORA_RANK)),
        "mla_w_uq": w((nm, Q_LORA_RANK, N_HEADS * (QK_NOPE_DIM + QK_ROPE_DIM)), Q_LORA_RANK),
        "mla_w_o": w((nm, N_HEADS * V_HEAD_DIM, D_MODEL), N_HEADS * V_HEAD_DIM, res_scale),
        "mla_ffn_norm_g": gain((nm, D_MODEL)),
        "mla_ffn_w13": w((nm, D_MODEL, 2 * D_FF), D_MODEL),
        "mla_ffn_w2": w((nm, D_FF, D_MODEL), D_FF, res_scale),
        "final_norm_g": gain((D_MODEL,)),
    }


def reference(x, positions,
              conv_norm_g, conv_w_in, conv_w, conv_w_out,
              conv_ffn_norm_g, conv_ffn_w13, conv_ffn_w2,
              kv_norm_g, w_dkv, kv_latent_norm_g, w_ukv,
              mla_norm_g, mla_w_dq, mla_q_norm_g, mla_w_uq, mla_w_o,
              mla_ffn_norm_g, mla_ffn_w13, mla_ffn_w2,
              final_norm_g):
    cos, sin = rope_tables(positions)
    cos_h, sin_h = cos[:, :, None, :], sin[:, :, None, :]
    h = x
    k_nope = k_pe = v = None
    for i in range(DEPTH):
        if i < N_CONV_LAYERS:
            h = h + short_conv_mixer(rms_norm(h, conv_norm_g[i]), conv_w_in[i], conv_w[i], conv_w_out[i])
            h = h + swiglu_ffn(rms_norm(h, conv_ffn_norm_g[i]), conv_ffn_w13[i], conv_ffn_w2[i])
        else:
            if i == N_CONV_LAYERS:
                k_nope, k_pe, v = shared_mla_kv(h, kv_norm_g, w_dkv, kv_latent_norm_g, w_ukv, cos, sin)
            j = i - N_CONV_LAYERS
            h = h + mla_mixer(rms_norm(h, mla_norm_g[j]), mla_w_dq[j], mla_q_norm_g[j], mla_w_uq[j],
                              mla_w_o[j], k_nope, k_pe, v, cos_h, sin_h)
            h = h + swiglu_ffn(rms_norm(h, mla_ffn_norm_g[j]), mla_ffn_w13[j], mla_ffn_w2[j])
    return rms_norm(h, final_norm_g)
```

```python
import functools

import jax
import jax.numpy as jnp
from jax import lax
from jax.experimental import pallas as pl
from jax.experimental.pallas import tpu as pltpu

N_HEADS = 8
QK_NOPE_DIM = 128
QK_ROPE_DIM = 64
V_HEAD_DIM = 128
CONV_WIDTH = 3
ROPE_THETA = 10000.0
RMS_EPS = 1e-6

LANES = 128
SUBLANES = 8
HEAD_PAD = 2 * LANES
VMEM_LIMIT_BYTES = 56 * 1024 * 1024

TOKEN_TILE = 512
ATTN_Q_TILE = 256
ATTN_HEADS_PER_STEP = 2
FFN_CHUNK = 1024

BF16 = jnp.bfloat16
F32 = jnp.float32


def _dot(a, b):
    return jnp.dot(a, b, preferred_element_type=F32)


def _rms_norm(x, g):
    ms = jnp.mean(x * x, axis=-1, keepdims=True)
    return x * lax.rsqrt(ms + RMS_EPS) * g


def _resident(shape):
    return pl.BlockSpec(shape, lambda *_: (0,) * len(shape),
                        pipeline_mode=pl.Buffered(1))


def _row_tile(tm, width):
    return pl.BlockSpec((tm, width), lambda i: (i, 0))


def _params(semantics):
    return pltpu.CompilerParams(dimension_semantics=semantics,
                                vmem_limit_bytes=VMEM_LIMIT_BYTES)


def _conv_mixer_kernel(h_ref, g_ref, w_in_ref, cw_ref, w_out_ref, o_ref,
                       carry_ref, *, tiles_per_seq):
    d = h_ref.shape[1]
    tm = h_ref.shape[0]

    @pl.when(pl.program_id(0) % tiles_per_seq == 0)
    def _():
        carry_ref[...] = jnp.zeros_like(carry_ref)

    h = h_ref[...]
    hn = _rms_norm(h, g_ref[...]).astype(BF16)
    gate_b = _dot(hn, w_in_ref[:, 0:d])
    gate_c = _dot(hn, w_in_ref[:, d:2 * d])
    val = _dot(hn, w_in_ref[:, 2 * d:3 * d])
    z = gate_c * val

    row = lax.broadcasted_iota(jnp.int32, z.shape, 0)
    prev1 = carry_ref[SUBLANES - 1:SUBLANES, :]
    prev2 = carry_ref[SUBLANES - 2:SUBLANES - 1, :]
    z1 = jnp.where(row == 0, prev1, pltpu.roll(z, 1, 0))
    z2 = jnp.where(row == 0, prev2,
                   jnp.where(row == 1, prev1, pltpu.roll(z, 2, 0)))
    cw = cw_ref[...]
    conv = cw[0:1, :] * z2 + cw[1:2, :] * z1 + cw[2:3, :] * z
    carry_ref[...] = z[tm - SUBLANES:tm, :]

    y = (gate_b * conv).astype(BF16)
    o_ref[...] = h + _dot(y, w_out_ref[...])


def _conv_mixer(h, norm_g, w_in, conv_w, w_out, *, seq_len, tm):
    t, d = h.shape
    kernel = functools.partial(_conv_mixer_kernel, tiles_per_seq=seq_len // tm)
    return pl.pallas_call(
        kernel,
        grid=(t // tm,),
        in_specs=[_row_tile(tm, d), _resident((1, d)), _resident(w_in.shape),
                  _resident(conv_w.shape), _resident(w_out.shape)],
        out_specs=_row_tile(tm, d),
        out_shape=jax.ShapeDtypeStruct((t, d), F32),
        scratch_shapes=[pltpu.VMEM((SUBLANES, d), F32)],
        compiler_params=_params(("arbitrary",)),
        name="conv_mixer",
    )(h, norm_g, w_in, conv_w, w_out)


def _ffn_kernel(*refs, has_attn, has_final, chunks):
    refs = list(refs)
    o_ref = refs.pop()
    h_ref = refs.pop(0)
    h = h_ref[...]
    if has_attn:
        a_ref, wo_ref = refs.pop(0), refs.pop(0)
        h = h + _dot(a_ref[...], wo_ref[...])
    g_ref, w13_ref, w2_ref = refs[0], refs[1], refs[2]
    d_ff = w2_ref.shape[0]

    hn = _rms_norm(h, g_ref[...]).astype(BF16)
    acc = h
    for c0, c1 in chunks:
        gate = _dot(hn, w13_ref[:, c0:c1])
        up = _dot(hn, w13_ref[:, d_ff + c0:d_ff + c1])
        act = gate * (1.0 / (1.0 + jnp.exp(-gate))) * up
        acc = acc + _dot(act.astype(BF16), w2_ref[c0:c1, :])
    if has_final:
        acc = _rms_norm(acc, refs[3][...])
    o_ref[...] = acc


def _ffn(h, norm_g, w13, w2, *, tm, attn=None, w_o=None, final_g=None):
    t, d = h.shape
    d_ff = w2.shape[0]
    chunks = tuple((c, min(c + FFN_CHUNK, d_ff)) for c in range(0, d_ff, FFN_CHUNK))
    has_attn, has_final = attn is not None, final_g is not None
    args, specs = [h], [_row_tile(tm, d)]
    if has_attn:
        args += [attn, w_o]
        specs += [_row_tile(tm, attn.shape[1]), _resident(w_o.shape)]
    args += [norm_g, w13, w2]
    specs += [_resident((1, d)), _resident(w13.shape), _resident(w2.shape)]
    if has_final:
        args.append(final_g)
        specs.append(_resident((1, d)))
    kernel = functools.partial(_ffn_kernel, has_attn=has_attn,
                               has_final=has_final, chunks=chunks)
    return pl.pallas_call(
        kernel,
        grid=(t // tm,),
        in_specs=specs,
        out_specs=_row_tile(tm, d),
        out_shape=jax.ShapeDtypeStruct((t, d), F32),
        compiler_params=_params(("parallel",)),
        name="ffn",
    )(*args)


def _rope_block(x, cs):
    t = x * cs
    return t + pltpu.roll(t, LANES // 2, 1)


def _shared_kv_kernel(h_ref, cs_ref, g_ref, w_dkv_ref, lat_g_ref, w_ukv_ref,
                      k_ref, v_ref, *, kv_lora):
    hn = _rms_norm(h_ref[...], g_ref[...]).astype(BF16)
    ckv = _dot(hn, w_dkv_ref[...])
    c_lat = _rms_norm(ckv[:, :kv_lora], lat_g_ref[...]).astype(BF16)
    pe = _rope_block(ckv[:, kv_lora:], cs_ref[...])
    lane = lax.broadcasted_iota(jnp.int32, pe.shape, 1)
    k_pe = jnp.where(lane < QK_ROPE_DIM, pe, 0.0).astype(BF16)
    kv = _dot(c_lat, w_ukv_ref[...])
    per_head = QK_NOPE_DIM + V_HEAD_DIM
    for hd in range(N_HEADS):
        k_ref[:, hd * HEAD_PAD:hd * HEAD_PAD + QK_NOPE_DIM] = (
            kv[:, hd * per_head:hd * per_head + QK_NOPE_DIM].astype(BF16))
        k_ref[:, hd * HEAD_PAD + QK_NOPE_DIM:(hd + 1) * HEAD_PAD] = k_pe
        v_ref[:, hd * V_HEAD_DIM:(hd + 1) * V_HEAD_DIM] = (
            kv[:, hd * per_head + QK_NOPE_DIM:(hd + 1) * per_head].astype(BF16))


def _shared_kv(h, cs, norm_g, w_dkv, lat_g, w_ukv, *, tm):
    t, d = h.shape
    kv_lora = lat_g.shape[1]
    kernel = functools.partial(_shared_kv_kernel, kv_lora=kv_lora)
    return pl.pallas_call(
        kernel,
        grid=(t // tm,),
        in_specs=[_row_tile(tm, d), _row_tile(tm, LANES), _resident((1, d)),
                  _resident(w_dkv.shape), _resident((1, kv_lora)),
                  _resident(w_ukv.shape)],
        out_specs=[_row_tile(tm, N_HEADS * HEAD_PAD),
                   _row_tile(tm, N_HEADS * V_HEAD_DIM)],
        out_shape=[jax.ShapeDtypeStruct((t, N_HEADS * HEAD_PAD), BF16),
                   jax.ShapeDtypeStruct((t, N_HEADS * V_HEAD_DIM), BF16)],
        compiler_params=_params(("parallel",)),
        name="shared_kv",
    )(h, cs, norm_g, w_dkv, lat_g, w_ukv)


def _q_proj_kernel(h_ref, cs_ref, g_ref, w_dq_ref, qg_ref, w_uq_ref, q_ref,
                   *, q_scale):
    hn = _rms_norm(h_ref[...], g_ref[...]).astype(BF16)
    cq = _rms_norm(_dot(hn, w_dq_ref[...]), qg_ref[...]).astype(BF16)
    q = _dot(cq, w_uq_ref[...])
    cs = cs_ref[...]
    for hd in range(N_HEADS):
        lo = hd * HEAD_PAD
        q_ref[:, lo:lo + QK_NOPE_DIM] = (
            q[:, lo:lo + QK_NOPE_DIM] * q_scale).astype(BF16)
        pe = _rope_block(q[:, lo + QK_NOPE_DIM:lo + HEAD_PAD], cs)
        q_ref[:, lo + QK_NOPE_DIM:lo + HEAD_PAD] = (pe * q_scale).astype(BF16)


def _q_proj(h, cs, norm_g, w_dq, q_norm_g, w_uq, *, tm, q_scale):
    t, d = h.shape
    q_lora = w_dq.shape[1]
    kernel = functools.partial(_q_proj_kernel, q_scale=q_scale)
    return pl.pallas_call(
        kernel,
        grid=(t // tm,),
        in_specs=[_row_tile(tm, d), _row_tile(tm, LANES), _resident((1, d)),
                  _resident(w_dq.shape), _resident((1, q_lora)),
                  _resident(w_uq.shape)],
        out_specs=_row_tile(tm, N_HEADS * HEAD_PAD),
        out_shape=jax.ShapeDtypeStruct((t, N_HEADS * HEAD_PAD), BF16),
        compiler_params=_params(("parallel",)),
        name="q_proj",
    )(h, cs, norm_g, w_dq, q_norm_g, w_uq)


def _attention_kernel(q_ref, k_ref, v_ref, o_ref, *, heads, tq):
    seq = q_ref.shape[0]
    row = lax.broadcasted_iota(jnp.int32, (tq, tq), 0)
    col = lax.broadcasted_iota(jnp.int32, (tq, tq), 1)
    causal = col <= row
    neg = jnp.finfo(F32).min
    for hd in range(heads):
        qc = slice(hd * HEAD_PAD, (hd + 1) * HEAD_PAD)
        vc = slice(hd * V_HEAD_DIM, (hd + 1) * V_HEAD_DIM)
        for off in range(0, seq, tq):
            kv_len = off + tq
            q = q_ref[off:kv_len, qc]
            k = k_ref[0:kv_len, qc]
            s = lax.dot_general(q, k, (((1,), (1,)), ((), ())),
                                preferred_element_type=F32)
            s_diag = jnp.where(causal, s[:, off:], neg)
            m = jnp.max(s_diag, axis=-1, keepdims=True)
            if off:
                s_past = s[:, :off]
                m = jnp.maximum(m, jnp.max(s_past, axis=-1, keepdims=True))
                e_past = jnp.exp2(s_past - m)
                e_diag = jnp.exp2(s_diag - m)
                denom = (jnp.sum(e_past, axis=-1, keepdims=True)
                         + jnp.sum(e_diag, axis=-1, keepdims=True))
                p = jnp.concatenate([e_past.astype(BF16), e_diag.astype(BF16)], axis=1)
            else:
                e_diag = jnp.exp2(s_diag - m)
                denom = jnp.sum(e_diag, axis=-1, keepdims=True)
                p = e_diag.astype(BF16)
            out = _dot(p, v_ref[0:kv_len, vc])
            o_ref[off:kv_len, vc] = (out / denom).astype(BF16)


def _attention(q, k, v, *, batch, seq_len):
    hp = ATTN_HEADS_PER_STEP
    kernel = functools.partial(_attention_kernel, heads=hp, tq=ATTN_Q_TILE)
    qk_spec = pl.BlockSpec((seq_len, hp * HEAD_PAD), lambda b, g: (b, g))
    v_spec = pl.BlockSpec((seq_len, hp * V_HEAD_DIM), lambda b, g: (b, g))
    return pl.pallas_call(
        kernel,
        grid=(batch, N_HEADS // hp),
        in_specs=[qk_spec, qk_spec, v_spec],
        out_specs=v_spec,
        out_shape=jax.ShapeDtypeStruct(v.shape, BF16),
        compiler_params=_params(("parallel", "parallel")),
        name="attention",
    )(q, k, v)


def _rope_columns(w_rope):
    half = QK_ROPE_DIM // 2
    x1, x2 = w_rope[..., :half], w_rope[..., half:]
    return jnp.concatenate([x1, x2, -x2, x1], axis=-1)


def _pad_q_weight(w_uq):
    q_lora = w_uq.shape[0]
    w = w_uq.reshape(q_lora, N_HEADS, QK_NOPE_DIM + QK_ROPE_DIM)
    w = jnp.concatenate([w[..., :QK_NOPE_DIM], _rope_columns(w[..., QK_NOPE_DIM:])], axis=-1)
    return w.reshape(q_lora, N_HEADS * HEAD_PAD)


def kernel(x, positions, conv_norm_g, conv_w_in, conv_w, conv_w_out, conv_ffn_norm_g, conv_ffn_w13, conv_ffn_w2, kv_norm_g, w_dkv, kv_latent_norm_g, w_ukv, mla_norm_g, mla_w_dq, mla_q_norm_g, mla_w_uq, mla_w_o, mla_ffn_norm_g, mla_ffn_w13, mla_ffn_w2, final_norm_g):
    batch, seq_len, d = x.shape
    t = batch * seq_len
    tm = TOKEN_TILE
    n_conv, n_mla = conv_w_in.shape[0], mla_w_dq.shape[0]
    kv_lora = kv_latent_norm_g.shape[0]
    assert seq_len % tm == 0 and seq_len % ATTN_Q_TILE == 0

    inv_freq = ROPE_THETA ** (-jnp.arange(0, QK_ROPE_DIM, 2, dtype=F32) / QK_ROPE_DIM)
    ang = positions.astype(F32).reshape(t, 1) * inv_freq
    cos, sin = jnp.cos(ang), jnp.sin(ang)
    cs = jnp.concatenate([cos, cos, sin, sin], axis=-1)

    row = lambda g: g.reshape(1, -1)
    h = x.reshape(t, d)

    for i in range(n_conv):
        h = _conv_mixer(h, row(conv_norm_g[i]), conv_w_in[i].astype(BF16), conv_w[i],
                        conv_w_out[i].astype(BF16), seq_len=seq_len, tm=tm)
        h = _ffn(h, row(conv_ffn_norm_g[i]), conv_ffn_w13[i].astype(BF16),
                 conv_ffn_w2[i].astype(BF16), tm=tm)

    w_dkv_p = jnp.concatenate([w_dkv[:, :kv_lora], _rope_columns(w_dkv[:, kv_lora:])], axis=-1)
    k, v = _shared_kv(h, cs, row(kv_norm_g), w_dkv_p.astype(BF16),
                      row(kv_latent_norm_g), w_ukv.astype(BF16), tm=tm)

    q_scale = float((QK_NOPE_DIM + QK_ROPE_DIM) ** -0.5 * 1.4426950408889634)
    for j in range(n_mla):
        q = _q_proj(h, cs, row(mla_norm_g[j]), mla_w_dq[j].astype(BF16),
                    row(mla_q_norm_g[j]), _pad_q_weight(mla_w_uq[j]).astype(BF16),
                    tm=tm, q_scale=q_scale)
        attn = _attention(q, k, v, batch=batch, seq_len=seq_len)
        last = j == n_mla - 1
        h = _ffn(h, row(mla_ffn_norm_g[j]), mla_ffn_w13[j].astype(BF16),
                 mla_ffn_w2[j].astype(BF16), tm=tm, attn=attn,
                 w_o=mla_w_o[j].astype(BF16),
                 final_g=row(final_norm_g) if last else None)
    return h.reshape(batch, seq_len, d)
```

```python
import functools

import jax
import jax.numpy as jnp
from jax import lax
from jax.experimental import pallas as pl
from jax.experimental.pallas import tpu as pltpu

N_HEADS = 8
QK_NOPE_DIM = 128
QK_ROPE_DIM = 64
V_HEAD_DIM = 128
CONV_WIDTH = 3
ROPE_THETA = 10000.0
RMS_EPS = 1e-6

LANES = 128
SUBLANES = 8
HEAD_PAD = 2 * LANES
VMEM_LIMIT_BYTES = 56 * 1024 * 1024

TOKEN_TILE = 512
ATTN_Q_TILE = 256
ATTN_HEADS_PER_STEP = 4
ATTN_LOOKAHEAD = 3
FFN_CHUNK = 1024

BF16 = jnp.bfloat16
F32 = jnp.float32


def _dot(a, b):
    return jnp.dot(a, b, preferred_element_type=F32)


def _rms_norm(x, g):
    ms = jnp.mean(x * x, axis=-1, keepdims=True)
    return x * lax.rsqrt(ms + RMS_EPS) * g


def _resident(shape):
    return pl.BlockSpec(shape, lambda *_: (0,) * len(shape),
                        pipeline_mode=pl.Buffered(1))


def _row_tile(tm, width):
    return pl.BlockSpec((tm, width), lambda i: (i, 0))


def _rope_tile(tm):
    return pl.BlockSpec((LANES, tm), lambda i: (0, i))


def _params(semantics):
    return pltpu.CompilerParams(dimension_semantics=semantics,
                                vmem_limit_bytes=VMEM_LIMIT_BYTES)


def _conv_mixer_kernel(h_ref, g_ref, w_in_ref, cw_ref, w_out_ref, o_ref,
                       carry_ref, *, tiles_per_seq):
    d = h_ref.shape[1]
    tm = h_ref.shape[0]

    @pl.when(pl.program_id(0) % tiles_per_seq == 0)
    def _():
        carry_ref[...] = jnp.zeros_like(carry_ref)

    h = h_ref[...]
    hn = _rms_norm(h, g_ref[...]).astype(BF16)
    gate_b = _dot(hn, w_in_ref[:, 0:d])
    gate_c = _dot(hn, w_in_ref[:, d:2 * d])
    val = _dot(hn, w_in_ref[:, 2 * d:3 * d])
    z = gate_c * val

    row = lax.broadcasted_iota(jnp.int32, z.shape, 0)
    prev1 = carry_ref[SUBLANES - 1:SUBLANES, :]
    prev2 = carry_ref[SUBLANES - 2:SUBLANES - 1, :]
    z1 = jnp.where(row == 0, prev1, pltpu.roll(z, 1, 0))
    z2 = jnp.where(row == 0, prev2,
                   jnp.where(row == 1, prev1, pltpu.roll(z, 2, 0)))
    cw = cw_ref[...]
    conv = cw[0:1, :] * z2 + cw[1:2, :] * z1 + cw[2:3, :] * z
    carry_ref[...] = z[tm - SUBLANES:tm, :]

    y = (gate_b * conv).astype(BF16)
    o_ref[...] = h + _dot(y, w_out_ref[...])


def _conv_mixer(h, norm_g, w_in, conv_w, w_out, *, seq_len, tm):
    t, d = h.shape
    kernel = functools.partial(_conv_mixer_kernel, tiles_per_seq=seq_len // tm)
    return pl.pallas_call(
        kernel,
        grid=(t // tm,),
        in_specs=[_row_tile(tm, d), _resident((1, d)), _resident(w_in.shape),
                  _resident(conv_w.shape), _resident(w_out.shape)],
        out_specs=_row_tile(tm, d),
        out_shape=jax.ShapeDtypeStruct((t, d), F32),
        scratch_shapes=[pltpu.VMEM((SUBLANES, d), F32)],
        compiler_params=_params(("arbitrary",)),
        name="conv_mixer",
    )(h, norm_g, w_in, conv_w, w_out)


def _ffn_kernel(*refs, has_attn, has_final, chunks):
    refs = list(refs)
    o_ref = refs.pop()
    h_ref = refs.pop(0)
    h = h_ref[...]
    if has_attn:
        a_ref, wo_ref = refs.pop(0), refs.pop(0)
        h = h + _dot(a_ref[...], wo_ref[...])
    g_ref, w13_ref, w2_ref = refs[0], refs[1], refs[2]
    d_ff = w2_ref.shape[0]

    hn = _rms_norm(h, g_ref[...]).astype(BF16)
    acc = h
    for c0, c1 in chunks:
        gate = _dot(hn, w13_ref[:, c0:c1])
        up = _dot(hn, w13_ref[:, d_ff + c0:d_ff + c1])
        act = gate * (1.0 / (1.0 + jnp.exp(-gate))) * up
        acc = acc + _dot(act.astype(BF16), w2_ref[c0:c1, :])
    if has_final:
        acc = _rms_norm(acc, refs[3][...])
    o_ref[...] = acc


def _ffn(h, norm_g, w13, w2, *, tm, attn=None, w_o=None, final_g=None):
    t, d = h.shape
    d_ff = w2.shape[0]
    chunks = tuple((c, min(c + FFN_CHUNK, d_ff)) for c in range(0, d_ff, FFN_CHUNK))
    has_attn, has_final = attn is not None, final_g is not None
    args, specs = [h], [_row_tile(tm, d)]
    if has_attn:
        args += [attn, w_o]
        specs += [_row_tile(tm, attn.shape[1]), _resident(w_o.shape)]
    args += [norm_g, w13, w2]
    specs += [_resident((1, d)), _resident(w13.shape), _resident(w2.shape)]
    if has_final:
        args.append(final_g)
        specs.append(_resident((1, d)))
    kernel = functools.partial(_ffn_kernel, has_attn=has_attn,
                               has_final=has_final, chunks=chunks)
    return pl.pallas_call(
        kernel,
        grid=(t // tm,),
        in_specs=specs,
        out_specs=_row_tile(tm, d),
        out_shape=jax.ShapeDtypeStruct((t, d), F32),
        compiler_params=_params(("parallel",)),
        name="ffn",
    )(*args)


def _rope_block(x, cs):
    t = x * cs
    return t + pltpu.roll(t, LANES // 2, 1)


def _shared_kv_kernel(h_ref, cs_ref, g_ref, w_dkv_ref, lat_g_ref, w_uk_ref,
                      w_uvt_ref, k_ref, vt_ref, *, kv_lora):
    hn = _rms_norm(h_ref[...], g_ref[...]).astype(BF16)
    ckv = _dot(hn, w_dkv_ref[...])
    c_lat = _rms_norm(ckv[:, :kv_lora], lat_g_ref[...]).astype(BF16)
    pe = _rope_block(ckv[:, kv_lora:], cs_ref[...].T)
    lane = lax.broadcasted_iota(jnp.int32, pe.shape, 1)
    k_pe = jnp.where(lane < QK_ROPE_DIM, pe, 0.0).astype(BF16)
    k_nope = _dot(c_lat, w_uk_ref[...])
    for hd in range(N_HEADS):
        k_ref[:, hd * HEAD_PAD:hd * HEAD_PAD + QK_NOPE_DIM] = (
            k_nope[:, hd * QK_NOPE_DIM:(hd + 1) * QK_NOPE_DIM].astype(BF16))
        k_ref[:, hd * HEAD_PAD + QK_NOPE_DIM:(hd + 1) * HEAD_PAD] = k_pe
    vt_ref[...] = lax.dot_general(w_uvt_ref[...], c_lat, (((1,), (1,)), ((), ())),
                                  preferred_element_type=F32).astype(BF16)


def _shared_kv(h, cs, norm_g, w_dkv, lat_g, w_uk, w_uvt, *, tm):
    t, d = h.shape
    kv_lora = lat_g.shape[1]
    kernel = functools.partial(_shared_kv_kernel, kv_lora=kv_lora)
    return pl.pallas_call(
        kernel,
        grid=(t // tm,),
        in_specs=[_row_tile(tm, d), _rope_tile(tm), _resident((1, d)),
                  _resident(w_dkv.shape), _resident((1, kv_lora)),
                  _resident(w_uk.shape), _resident(w_uvt.shape)],
        out_specs=[_row_tile(tm, N_HEADS * HEAD_PAD),
                   pl.BlockSpec((N_HEADS * V_HEAD_DIM, tm), lambda i: (0, i))],
        out_shape=[jax.ShapeDtypeStruct((t, N_HEADS * HEAD_PAD), BF16),
                   jax.ShapeDtypeStruct((N_HEADS * V_HEAD_DIM, t), BF16)],
        compiler_params=_params(("parallel",)),
        name="shared_kv",
    )(h, cs, norm_g, w_dkv, lat_g, w_uk, w_uvt)


def _q_proj_kernel(h_ref, cs_ref, g_ref, w_dq_ref, qg_ref, w_uq_ref, q_ref,
                   *, q_scale):
    hn = _rms_norm(h_ref[...], g_ref[...]).astype(BF16)
    cq = _rms_norm(_dot(hn, w_dq_ref[...]), qg_ref[...]).astype(BF16)
    q = _dot(cq, w_uq_ref[...])
    cs = cs_ref[...].T
    for hd in range(N_HEADS):
        lo = hd * HEAD_PAD
        q_ref[:, lo:lo + QK_NOPE_DIM] = (
            q[:, lo:lo + QK_NOPE_DIM] * q_scale).astype(BF16)
        pe = _rope_block(q[:, lo + QK_NOPE_DIM:lo + HEAD_PAD], cs)
        q_ref[:, lo + QK_NOPE_DIM:lo + HEAD_PAD] = (pe * q_scale).astype(BF16)


def _q_proj(h, cs, norm_g, w_dq, q_norm_g, w_uq, *, tm, q_scale):
    t, d = h.shape
    q_lora = w_dq.shape[1]
    kernel = functools.partial(_q_proj_kernel, q_scale=q_scale)
    return pl.pallas_call(
        kernel,
        grid=(t // tm,),
        in_specs=[_row_tile(tm, d), _rope_tile(tm), _resident((1, d)),
                  _resident(w_dq.shape), _resident((1, q_lora)),
                  _resident(w_uq.shape)],
        out_specs=_row_tile(tm, N_HEADS * HEAD_PAD),
        out_shape=jax.ShapeDtypeStruct((t, N_HEADS * HEAD_PAD), BF16),
        compiler_params=_params(("parallel",)),
        name="q_proj",
    )(h, cs, norm_g, w_dq, q_norm_g, w_uq)


def _attention_kernel(q_ref, k_ref, vt_ref, o_ref, *, heads, tq):
    seq = q_ref.shape[0]
    key = lax.broadcasted_iota(jnp.int32, (tq, tq), 0)
    qry = lax.broadcasted_iota(jnp.int32, (tq, tq), 1)
    causal = key <= qry
    neg = jnp.finfo(F32).min

    def scores(hd, off):
        qc = slice(hd * HEAD_PAD, (hd + 1) * HEAD_PAD)
        q = q_ref[off:off + tq, qc]
        k = k_ref[0:off + tq, qc]
        return lax.dot_general(k, q, (((1,), (1,)), ((), ())),
                               preferred_element_type=F32)

    def finish(hd, off, st):
        vc = slice(hd * V_HEAD_DIM, (hd + 1) * V_HEAD_DIM)
        s_diag = jnp.where(causal, st[off:, :], neg)
        m = jnp.max(s_diag, axis=0, keepdims=True)
        if off:
            s_past = st[:off, :]
            m = jnp.maximum(m, jnp.max(s_past, axis=0, keepdims=True))
            e_past = jnp.exp2(s_past - m)
            e_diag = jnp.exp2(s_diag - m)
            denom = (jnp.sum(e_past, axis=0, keepdims=True)
                     + jnp.sum(e_diag, axis=0, keepdims=True))
            pt = jnp.concatenate([e_past.astype(BF16), e_diag.astype(BF16)], axis=0)
        else:
            e_diag = jnp.exp2(s_diag - m)
            denom = jnp.sum(e_diag, axis=0, keepdims=True)
            pt = e_diag.astype(BF16)
        out_t = _dot(vt_ref[vc, 0:off + tq], pt)
        o_ref[off:off + tq, vc] = (out_t / denom).T.astype(BF16)

    tiles = [(hd, off) for off in range(0, seq, tq) for hd in range(heads)]
    pending = [scores(*tile) for tile in tiles[:ATTN_LOOKAHEAD]]
    for n, tile in enumerate(tiles):
        if n + ATTN_LOOKAHEAD < len(tiles):
            pending.append(scores(*tiles[n + ATTN_LOOKAHEAD]))
        finish(*tile, pending.pop(0))


def _attention(q, k, vt, *, batch, seq_len):
    hp = ATTN_HEADS_PER_STEP
    kernel = functools.partial(_attention_kernel, heads=hp, tq=ATTN_Q_TILE)
    qk_spec = pl.BlockSpec((seq_len, hp * HEAD_PAD), lambda b, g: (b, g))
    vt_spec = pl.BlockSpec((hp * V_HEAD_DIM, seq_len), lambda b, g: (g, b))
    out_spec = pl.BlockSpec((seq_len, hp * V_HEAD_DIM), lambda b, g: (b, g))
    return pl.pallas_call(
        kernel,
        grid=(batch, N_HEADS // hp),
        in_specs=[qk_spec, qk_spec, vt_spec],
        out_specs=out_spec,
        out_shape=jax.ShapeDtypeStruct((batch * seq_len, N_HEADS * V_HEAD_DIM), BF16),
        compiler_params=_params(("parallel", "parallel")),
        name="attention",
    )(q, k, vt)


def _rope_columns(w_rope):
    half = QK_ROPE_DIM // 2
    x1, x2 = w_rope[..., :half], w_rope[..., half:]
    return jnp.concatenate([x1, x2, -x2, x1], axis=-1)


def _pad_q_weight(w_uq):
    q_lora = w_uq.shape[0]
    w = w_uq.reshape(q_lora, N_HEADS, QK_NOPE_DIM + QK_ROPE_DIM)
    w = jnp.concatenate([w[..., :QK_NOPE_DIM], _rope_columns(w[..., QK_NOPE_DIM:])], axis=-1)
    return w.reshape(q_lora, N_HEADS * HEAD_PAD)


def kernel(x, positions, conv_norm_g, conv_w_in, conv_w, conv_w_out, conv_ffn_norm_g, conv_ffn_w13, conv_ffn_w2, kv_norm_g, w_dkv, kv_latent_norm_g, w_ukv, mla_norm_g, mla_w_dq, mla_q_norm_g, mla_w_uq, mla_w_o, mla_ffn_norm_g, mla_ffn_w13, mla_ffn_w2, final_norm_g):
    batch, seq_len, d = x.shape
    t = batch * seq_len
    tm = TOKEN_TILE
    n_conv, n_mla = conv_w_in.shape[0], mla_w_dq.shape[0]
    kv_lora = kv_latent_norm_g.shape[0]
    assert seq_len % tm == 0 and seq_len % ATTN_Q_TILE == 0

    inv_freq = ROPE_THETA ** (-jnp.arange(0, QK_ROPE_DIM, 2, dtype=F32) / QK_ROPE_DIM)
    ang = positions.astype(F32).reshape(1, t) * inv_freq.reshape(-1, 1)
    cos, sin = jnp.cos(ang), jnp.sin(ang)
    cs = jnp.concatenate([cos, cos, sin, sin], axis=0)

    row = lambda g: g.reshape(1, -1)
    h = x.reshape(t, d)

    for i in range(n_conv):
        h = _conv_mixer(h, row(conv_norm_g[i]), conv_w_in[i].astype(BF16), conv_w[i],
                        conv_w_out[i].astype(BF16), seq_len=seq_len, tm=tm)
        h = _ffn(h, row(conv_ffn_norm_g[i]), conv_ffn_w13[i].astype(BF16),
                 conv_ffn_w2[i].astype(BF16), tm=tm)

    w_dkv_p = jnp.concatenate([w_dkv[:, :kv_lora], _rope_columns(w_dkv[:, kv_lora:])], axis=-1)
    w_ukv_h = w_ukv.reshape(kv_lora, N_HEADS, QK_NOPE_DIM + V_HEAD_DIM)
    w_uk = w_ukv_h[..., :QK_NOPE_DIM].reshape(kv_lora, N_HEADS * QK_NOPE_DIM)
    w_uvt = w_ukv_h[..., QK_NOPE_DIM:].reshape(kv_lora, N_HEADS * V_HEAD_DIM).T
    k, vt = _shared_kv(h, cs, row(kv_norm_g), w_dkv_p.astype(BF16),
                       row(kv_latent_norm_g), w_uk.astype(BF16),
                       w_uvt.astype(BF16), tm=tm)

    q_scale = float((QK_NOPE_DIM + QK_ROPE_DIM) ** -0.5 * 1.4426950408889634)
    for j in range(n_mla):
        q = _q_proj(h, cs, row(mla_norm_g[j]), mla_w_dq[j].astype(BF16),
                    row(mla_q_norm_g[j]), _pad_q_weight(mla_w_uq[j]).astype(BF16),
                    tm=tm, q_scale=q_scale)
        attn = _attention(q, k, vt, batch=batch, seq_len=seq_len)
        last = j == n_mla - 1
        h = _ffn(h, row(mla_ffn_norm_g[j]), mla_ffn_w13[j].astype(BF16),
                 mla_ffn_w2[j].astype(BF16), tm=tm, attn=attn,
                 w_o=mla_w_o[j].astype(BF16),
                 final_g=row(final_norm_g) if last else None)
    return h.reshape(batch, seq_len, d)
```

```python
import functools

import jax
import jax.numpy as jnp
from jax import lax
from jax.experimental import pallas as pl
from jax.experimental.pallas import tpu as pltpu

N_HEADS = 8
QK_NOPE_DIM = 128
QK_ROPE_DIM = 64
V_HEAD_DIM = 128
CONV_WIDTH = 3
ROPE_THETA = 10000.0
RMS_EPS = 1e-6

LANES = 128
SUBLANES = 8
HEAD_PAD = 2 * LANES
BF16_ROWS = 16
VT_ROWS = V_HEAD_DIM + BF16_ROWS
VMEM_LIMIT_BYTES = 56 * 1024 * 1024

TOKEN_TILE = 512
ATTN_Q_TILE = 256
ATTN_HEADS_PER_STEP = 4
ATTN_LOOKAHEAD = 3
FFN_CHUNK = 1024
FFN_SUB_ROWS = 256

BF16 = jnp.bfloat16
F32 = jnp.float32


def _dot(a, b):
    return jnp.dot(a, b, preferred_element_type=F32)


def _rms_norm(x, g):
    ms = jnp.mean(x * x, axis=-1, keepdims=True)
    return x * lax.rsqrt(ms + RMS_EPS) * g


def _resident(shape):
    return pl.BlockSpec(shape, lambda *_: (0,) * len(shape),
                        pipeline_mode=pl.Buffered(1))


def _row_tile(tm, width):
    return pl.BlockSpec((tm, width), lambda i: (i, 0))


def _rope_tile(tm):
    return pl.BlockSpec((LANES, tm), lambda i: (0, i))


def _params(semantics):
    return pltpu.CompilerParams(dimension_semantics=semantics,
                                vmem_limit_bytes=VMEM_LIMIT_BYTES)


def _conv_mixer_kernel(h_ref, g_ref, w_in_ref, cw_ref, w_out_ref, o_ref,
                       carry_ref, *, tiles_per_seq, sub_rows):
    d = h_ref.shape[1]
    tm = h_ref.shape[0]

    @pl.when(pl.program_id(0) % tiles_per_seq == 0)
    def _():
        carry_ref[...] = jnp.zeros_like(carry_ref)

    subs = [slice(r, r + sub_rows) for r in range(0, tm, sub_rows)]
    hs = [h_ref[rs, :] for rs in subs]
    hns = [_rms_norm(h, g_ref[...]).astype(BF16) for h in hs]
    gates_c = [_dot(hn, w_in_ref[:, d:2 * d]) for hn in hns]
    vals = [_dot(hn, w_in_ref[:, 2 * d:3 * d]) for hn in hns]
    gates_b = [_dot(hn, w_in_ref[:, 0:d]) for hn in hns]

    row = lax.broadcasted_iota(jnp.int32, (sub_rows, d), 0)
    cw = cw_ref[...]
    tail = carry_ref[...]
    for rs, h, gate_b, gate_c, val in zip(subs, hs, gates_b, gates_c, vals):
        z = gate_c * val
        prev1 = tail[SUBLANES - 1:SUBLANES, :]
        prev2 = tail[SUBLANES - 2:SUBLANES - 1, :]
        z1 = jnp.where(row == 0, prev1, pltpu.roll(z, 1, 0))
        z2 = jnp.where(row == 0, prev2,
                       jnp.where(row == 1, prev1, pltpu.roll(z, 2, 0)))
        conv = cw[0:1, :] * z2 + cw[1:2, :] * z1 + cw[2:3, :] * z
        tail = z[sub_rows - SUBLANES:sub_rows, :]
        y = (gate_b * conv).astype(BF16)
        o_ref[rs, :] = h + _dot(y, w_out_ref[...])
    carry_ref[...] = tail


def _conv_mixer(h, norm_g, w_in, conv_w, w_out, *, seq_len, tm):
    t, d = h.shape
    kernel = functools.partial(_conv_mixer_kernel, tiles_per_seq=seq_len // tm,
                               sub_rows=FFN_SUB_ROWS)
    return pl.pallas_call(
        kernel,
        grid=(t // tm,),
        in_specs=[_row_tile(tm, d), _resident((1, d)), _resident(w_in.shape),
                  _resident(conv_w.shape), _resident(w_out.shape)],
        out_specs=_row_tile(tm, d),
        out_shape=jax.ShapeDtypeStruct((t, d), F32),
        scratch_shapes=[pltpu.VMEM((SUBLANES, d), F32)],
        compiler_params=_params(("arbitrary",)),
        name="conv_mixer",
    )(h, norm_g, w_in, conv_w, w_out)


def _ffn_kernel(*refs, has_attn, has_final, chunks, sub_rows):
    refs = list(refs)
    o_ref = refs.pop()
    h_ref = refs.pop(0)
    if has_attn:
        a_ref, wo_ref = refs.pop(0), refs.pop(0)
    g_ref, w13_ref, w2_ref = refs[0], refs[1], refs[2]
    d_ff = w2_ref.shape[0]

    subs = [slice(r, r + sub_rows) for r in range(0, h_ref.shape[0], sub_rows)]
    accs, hns = [], []
    for rs in subs:
        h = h_ref[rs, :]
        if has_attn:
            h = h + _dot(a_ref[rs, :], wo_ref[...])
        accs.append(h)
        hns.append(_rms_norm(h, g_ref[...]).astype(BF16))
    for c0, c1 in chunks:
        gates = [_dot(hn, w13_ref[:, c0:c1]) for hn in hns]
        ups = [_dot(hn, w13_ref[:, d_ff + c0:d_ff + c1]) for hn in hns]
        for i, (gate, up) in enumerate(zip(gates, ups)):
            act = gate * (1.0 / (1.0 + jnp.exp(-gate))) * up
            accs[i] = accs[i] + _dot(act.astype(BF16), w2_ref[c0:c1, :])
    for rs, acc in zip(subs, accs):
        if has_final:
            acc = _rms_norm(acc, refs[3][...])
        o_ref[rs, :] = acc


def _ffn(h, norm_g, w13, w2, *, tm, attn=None, w_o=None, final_g=None):
    t, d = h.shape
    d_ff = w2.shape[0]
    chunks = tuple((c, min(c + FFN_CHUNK, d_ff)) for c in range(0, d_ff, FFN_CHUNK))
    has_attn, has_final = attn is not None, final_g is not None
    args, specs = [h], [_row_tile(tm, d)]
    if has_attn:
        args += [attn, w_o]
        specs += [_row_tile(tm, attn.shape[1]), _resident(w_o.shape)]
    args += [norm_g, w13, w2]
    specs += [_resident((1, d)), _resident(w13.shape), _resident(w2.shape)]
    if has_final:
        args.append(final_g)
        specs.append(_resident((1, d)))
    kernel = functools.partial(_ffn_kernel, has_attn=has_attn, has_final=has_final,
                               chunks=chunks, sub_rows=FFN_SUB_ROWS)
    return pl.pallas_call(
        kernel,
        grid=(t // tm,),
        in_specs=specs,
        out_specs=_row_tile(tm, d),
        out_shape=jax.ShapeDtypeStruct((t, d), F32),
        compiler_params=_params(("parallel",)),
        name="ffn",
    )(*args)


def _rope_block(x, cs):
    t = x * cs
    return t + pltpu.roll(t, LANES // 2, 1)


def _store_q(q_ref, rs, q, cs, q_scale):
    for hd in range(N_HEADS):
        lo = hd * HEAD_PAD
        q_ref[rs, lo:lo + QK_NOPE_DIM] = (
            q[:, lo:lo + QK_NOPE_DIM] * q_scale).astype(BF16)
        pe = _rope_block(q[:, lo + QK_NOPE_DIM:lo + HEAD_PAD], cs)
        q_ref[rs, lo + QK_NOPE_DIM:lo + HEAD_PAD] = (pe * q_scale).astype(BF16)


def _kv_q_kernel(*refs, kv_lora, q_scale, with_kv, sub_rows):
    if with_kv:
        (h_ref, cs_ref, qn_g_ref, w_dq_ref, qg_ref, w_uq_ref, kv_g_ref, w_dkv_ref,
         lat_g_ref, w_uk_ref, w_uvt_ref, q_ref, k_ref, vt_ref) = refs
    else:
        h_ref, cs_ref, qn_g_ref, w_dq_ref, qg_ref, w_uq_ref, q_ref = refs
    subs = [slice(r, r + sub_rows) for r in range(0, h_ref.shape[0], sub_rows)]
    css = [cs_ref[:, rs].T for rs in subs]

    hns_q, hns_kv = [], []
    for rs in subs:
        h = h_ref[rs, :]
        unit = h * lax.rsqrt(jnp.mean(h * h, axis=-1, keepdims=True) + RMS_EPS)
        hns_q.append((unit * qn_g_ref[...]).astype(BF16))
        if with_kv:
            hns_kv.append((unit * kv_g_ref[...]).astype(BF16))

    cqs = [_dot(hn, w_dq_ref[...]) for hn in hns_q]
    ckvs = [_dot(hn, w_dkv_ref[...]) for hn in hns_kv]
    cqns = [_rms_norm(cq, qg_ref[...]).astype(BF16) for cq in cqs]
    qs = [_dot(cqn, w_uq_ref[...]) for cqn in cqns]
    if with_kv:
        c_lats = [_rms_norm(ckv[:, :kv_lora], lat_g_ref[...]).astype(BF16)
                  for ckv in ckvs]
        k_nopes = [_dot(c_lat, w_uk_ref[...]) for c_lat in c_lats]
        vts = [lax.dot_general(w_uvt_ref[...], c_lat, (((1,), (1,)), ((), ())),
                               preferred_element_type=F32) for c_lat in c_lats]
    for i, rs in enumerate(subs):
        _store_q(q_ref, rs, qs[i], css[i], q_scale)
    if with_kv:
        lane = lax.broadcasted_iota(jnp.int32, (sub_rows, LANES), 1)
        for i, rs in enumerate(subs):
            pe = _rope_block(ckvs[i][:, kv_lora:], css[i])
            k_pe = jnp.where(lane < QK_ROPE_DIM, pe, 0.0).astype(BF16)
            for hd in range(N_HEADS):
                k_ref[rs, hd * HEAD_PAD:hd * HEAD_PAD + QK_NOPE_DIM] = (
                    k_nopes[i][:, hd * QK_NOPE_DIM:(hd + 1) * QK_NOPE_DIM].astype(BF16))
                k_ref[rs, hd * HEAD_PAD + QK_NOPE_DIM:(hd + 1) * HEAD_PAD] = k_pe
            ones = jnp.ones((VT_ROWS - V_HEAD_DIM, sub_rows), BF16)
            for hd in range(N_HEADS):
                vt_ref[hd * VT_ROWS:hd * VT_ROWS + V_HEAD_DIM, rs] = (
                    vts[i][hd * V_HEAD_DIM:(hd + 1) * V_HEAD_DIM, :].astype(BF16))
                vt_ref[hd * VT_ROWS + V_HEAD_DIM:(hd + 1) * VT_ROWS, rs] = ones


def _kv_q(h, cs, q_weights, kv_weights, *, tm, q_scale):
    t, d = h.shape
    with_kv = kv_weights is not None
    args = [h, cs, *q_weights]
    specs = [_row_tile(tm, d), _rope_tile(tm)] + [_resident(w.shape) for w in q_weights]
    out_specs = [_row_tile(tm, N_HEADS * HEAD_PAD)]
    out_shape = [jax.ShapeDtypeStruct((t, N_HEADS * HEAD_PAD), BF16)]
    kv_lora = 0
    if with_kv:
        kv_lora = kv_weights[2].shape[1]
        args += list(kv_weights)
        specs += [_resident(w.shape) for w in kv_weights]
        out_specs += [_row_tile(tm, N_HEADS * HEAD_PAD),
                      pl.BlockSpec((N_HEADS * VT_ROWS, tm), lambda i: (0, i))]
        out_shape += [jax.ShapeDtypeStruct((t, N_HEADS * HEAD_PAD), BF16),
                      jax.ShapeDtypeStruct((N_HEADS * VT_ROWS, t), BF16)]
    kernel = functools.partial(_kv_q_kernel, kv_lora=kv_lora, q_scale=q_scale,
                               with_kv=with_kv, sub_rows=FFN_SUB_ROWS)
    out = pl.pallas_call(
        kernel,
        grid=(t // tm,),
        in_specs=specs,
        out_specs=out_specs,
        out_shape=out_shape,
        compiler_params=_params(("parallel",)),
        name="kv_q" if with_kv else "q_proj",
    )(*args)
    return out if with_kv else out[0]


def _attention_kernel(q_ref, k_ref, vt_ref, o_ref, *, heads, tq):
    seq = q_ref.shape[0]
    key = lax.broadcasted_iota(jnp.int32, (tq, tq), 0)
    qry = lax.broadcasted_iota(jnp.int32, (tq, tq), 1)
    causal = key <= qry
    neg = jnp.finfo(F32).min

    def scores(hd, off):
        qc = slice(hd * HEAD_PAD, (hd + 1) * HEAD_PAD)
        q = q_ref[off:off + tq, qc]
        k = k_ref[0:off + tq, qc]
        return lax.dot_general(k, q, (((1,), (1,)), ((), ())),
                               preferred_element_type=F32)

    def finish(hd, off, st):
        vc = slice(hd * V_HEAD_DIM, (hd + 1) * V_HEAD_DIM)
        s_diag = jnp.where(causal, st[off:, :], neg)
        m = jnp.max(s_diag, axis=0, keepdims=True)
        if off:
            s_past = st[:off, :]
            m = jnp.maximum(m, jnp.max(s_past, axis=0, keepdims=True))
            pt = jnp.concatenate([jnp.exp2(s_past - m).astype(BF16),
                                  jnp.exp2(s_diag - m).astype(BF16)], axis=0)
        else:
            pt = jnp.exp2(s_diag - m).astype(BF16)
        vr = slice(hd * VT_ROWS, (hd + 1) * VT_ROWS)
        out_t = _dot(vt_ref[vr, 0:off + tq], pt)
        denom = out_t[V_HEAD_DIM:V_HEAD_DIM + 1, :]
        o_ref[off:off + tq, vc] = (out_t[:V_HEAD_DIM, :] / denom).T.astype(BF16)

    tiles = [(hd, off) for off in range(0, seq, tq) for hd in range(heads)]
    pending = [scores(*tile) for tile in tiles[:ATTN_LOOKAHEAD]]
    for n, tile in enumerate(tiles):
        if n + ATTN_LOOKAHEAD < len(tiles):
            pending.append(scores(*tiles[n + ATTN_LOOKAHEAD]))
        finish(*tile, pending.pop(0))


def _attention(q, k, vt, *, batch, seq_len):
    hp = ATTN_HEADS_PER_STEP
    kernel = functools.partial(_attention_kernel, heads=hp, tq=ATTN_Q_TILE)
    qk_spec = pl.BlockSpec((seq_len, hp * HEAD_PAD), lambda b, g: (b, g))
    vt_spec = pl.BlockSpec((hp * VT_ROWS, seq_len), lambda b, g: (g, b))
    out_spec = pl.BlockSpec((seq_len, hp * V_HEAD_DIM), lambda b, g: (b, g))
    return pl.pallas_call(
        kernel,
        grid=(batch, N_HEADS // hp),
        in_specs=[qk_spec, qk_spec, vt_spec],
        out_specs=out_spec,
        out_shape=jax.ShapeDtypeStruct((batch * seq_len, N_HEADS * V_HEAD_DIM), BF16),
        compiler_params=_params(("parallel", "parallel")),
        name="attention",
    )(q, k, vt)


def _rope_columns(w_rope):
    half = QK_ROPE_DIM // 2
    x1, x2 = w_rope[..., :half], w_rope[..., half:]
    return jnp.concatenate([x1, x2, -x2, x1], axis=-1)


def _pad_q_weight(w_uq):
    q_lora = w_uq.shape[0]
    w = w_uq.reshape(q_lora, N_HEADS, QK_NOPE_DIM + QK_ROPE_DIM)
    w = jnp.concatenate([w[..., :QK_NOPE_DIM], _rope_columns(w[..., QK_NOPE_DIM:])], axis=-1)
    return w.reshape(q_lora, N_HEADS * HEAD_PAD)


def kernel(x, positions, conv_norm_g, conv_w_in, conv_w, conv_w_out, conv_ffn_norm_g, conv_ffn_w13, conv_ffn_w2, kv_norm_g, w_dkv, kv_latent_norm_g, w_ukv, mla_norm_g, mla_w_dq, mla_q_norm_g, mla_w_uq, mla_w_o, mla_ffn_norm_g, mla_ffn_w13, mla_ffn_w2, final_norm_g):
    batch, seq_len, d = x.shape
    t = batch * seq_len
    tm = TOKEN_TILE
    n_conv, n_mla = conv_w_in.shape[0], mla_w_dq.shape[0]
    kv_lora = kv_latent_norm_g.shape[0]
    assert seq_len % tm == 0 and seq_len % ATTN_Q_TILE == 0

    inv_freq = ROPE_THETA ** (-jnp.arange(0, QK_ROPE_DIM, 2, dtype=F32) / QK_ROPE_DIM)
    ang = positions.astype(F32).reshape(1, t) * inv_freq.reshape(-1, 1)
    cos, sin = jnp.cos(ang), jnp.sin(ang)
    cs = jnp.concatenate([cos, cos, sin, sin], axis=0)

    row = lambda g: g.reshape(1, -1)
    h = x.reshape(t, d)

    for i in range(n_conv):
        h = _conv_mixer(h, row(conv_norm_g[i]), conv_w_in[i].astype(BF16), conv_w[i],
                        conv_w_out[i].astype(BF16), seq_len=seq_len, tm=tm)
        h = _ffn(h, row(conv_ffn_norm_g[i]), conv_ffn_w13[i].astype(BF16),
                 conv_ffn_w2[i].astype(BF16), tm=tm)

    w_dkv_p = jnp.concatenate([w_dkv[:, :kv_lora], _rope_columns(w_dkv[:, kv_lora:])], axis=-1)
    w_ukv_h = w_ukv.reshape(kv_lora, N_HEADS, QK_NOPE_DIM + V_HEAD_DIM)
    w_uk = w_ukv_h[..., :QK_NOPE_DIM].reshape(kv_lora, N_HEADS * QK_NOPE_DIM)
    w_uvt = w_ukv_h[..., QK_NOPE_DIM:].reshape(kv_lora, N_HEADS * V_HEAD_DIM).T
    kv_weights = (row(kv_norm_g), w_dkv_p.astype(BF16), row(kv_latent_norm_g),
                  w_uk.astype(BF16), w_uvt.astype(BF16))

    q_scale = float((QK_NOPE_DIM + QK_ROPE_DIM) ** -0.5 * 1.4426950408889634)
    for j in range(n_mla):
        q_weights = (row(mla_norm_g[j]), mla_w_dq[j].astype(BF16),
                     row(mla_q_norm_g[j]), _pad_q_weight(mla_w_uq[j]).astype(BF16))
        if j == 0:
            q, k, vt = _kv_q(h, cs, q_weights, kv_weights, tm=tm, q_scale=q_scale)
        else:
            q = _kv_q(h, cs, q_weights, None, tm=tm, q_scale=q_scale)
        attn = _attention(q, k, vt, batch=batch, seq_len=seq_len)
        last = j == n_mla - 1
        h = _ffn(h, row(mla_ffn_norm_g[j]), mla_ffn_w13[j].astype(BF16),
                 mla_ffn_w2[j].astype(BF16), tm=tm, attn=attn,
                 w_o=mla_w_o[j].astype(BF16),
                 final_g=row(final_norm_g) if last else None)
    return h.reshape(batch, seq_len, d)
```

```python
import functools

import jax
import jax.numpy as jnp
from jax import lax
from jax.experimental import pallas as pl
from jax.experimental.pallas import tpu as pltpu

N_HEADS = 8
QK_NOPE_DIM = 128
QK_ROPE_DIM = 64
V_HEAD_DIM = 128
CONV_WIDTH = 3
ROPE_THETA = 10000.0
RMS_EPS = 1e-6

LANES = 128
SUBLANES = 8
HEAD_PAD = 2 * LANES
BF16_ROWS = 16
VT_ROWS = V_HEAD_DIM + BF16_ROWS
VMEM_LIMIT_BYTES = 56 * 1024 * 1024

TOKEN_TILE = 512
ATTN_Q_TILE = 256
ATTN_HEADS_PER_STEP = 4
ATTN_LOOKAHEAD = 3
FFN_CHUNK = 1024
FFN_SUB_ROWS = 256
WEIGHT_CAST_STEPS = 8

BF16 = jnp.bfloat16
F32 = jnp.float32


def _dot(a, b):
    return jnp.dot(a, b, preferred_element_type=F32)


def _rms_norm(x, g):
    ms = jnp.mean(x * x, axis=-1, keepdims=True)
    return x * lax.rsqrt(ms + RMS_EPS) * g


def _resident(shape):
    return pl.BlockSpec(shape, lambda *_: (0,) * len(shape),
                        pipeline_mode=pl.Buffered(1))


def _row_tile(tm, width):
    return pl.BlockSpec((tm, width), lambda i: (i, 0))


def _rope_tile(tm):
    return pl.BlockSpec((LANES, tm), lambda i: (0, i))


def _params(semantics):
    return pltpu.CompilerParams(dimension_semantics=semantics,
                                vmem_limit_bytes=VMEM_LIMIT_BYTES)


def _weight_chunk(w_stacked, layer):
    _, rows, cols = w_stacked.shape
    last = WEIGHT_CAST_STEPS - 1
    return pl.BlockSpec((None, rows // WEIGHT_CAST_STEPS, cols),
                        lambda i: (layer, jnp.minimum(i, last), 0))


def _tile_after_cast(tm, width):
    return pl.BlockSpec((tm, width),
                        lambda i: (jnp.maximum(i - WEIGHT_CAST_STEPS, 0), 0))


def _cast_chunk(step, src_ref, dst_ref):
    rows = src_ref.shape[0]
    start = pl.multiple_of(step * rows, rows)
    dst_ref[pl.ds(start, rows), :] = src_ref[...].astype(BF16)


def _bf16_scratch(w_stacked):
    return pltpu.VMEM(w_stacked.shape[1:], BF16)


def _conv_mixer_kernel(h_ref, g_ref, w_in_f32_ref, cw_ref, w_out_f32_ref, o_ref,
                       w_in_ref, w_out_ref, carry_ref, *, tiles_per_seq, sub_rows):
    d = h_ref.shape[1]
    tm = h_ref.shape[0]
    step = pl.program_id(0)

    @pl.when(step < WEIGHT_CAST_STEPS)
    def _():
        _cast_chunk(step, w_in_f32_ref, w_in_ref)
        _cast_chunk(step, w_out_f32_ref, w_out_ref)

    @pl.when(step >= WEIGHT_CAST_STEPS)
    def _():
        @pl.when((step - WEIGHT_CAST_STEPS) % tiles_per_seq == 0)
        def _():
            carry_ref[...] = jnp.zeros_like(carry_ref)

        subs = [slice(r, r + sub_rows) for r in range(0, tm, sub_rows)]
        hs = [h_ref[rs, :] for rs in subs]
        hns = [_rms_norm(h, g_ref[...]).astype(BF16) for h in hs]
        gates_c = [_dot(hn, w_in_ref[:, d:2 * d]) for hn in hns]
        vals = [_dot(hn, w_in_ref[:, 2 * d:3 * d]) for hn in hns]
        gates_b = [_dot(hn, w_in_ref[:, 0:d]) for hn in hns]

        row = lax.broadcasted_iota(jnp.int32, (sub_rows, d), 0)
        cw = cw_ref[...]
        tail = carry_ref[...]
        for rs, h, gate_b, gate_c, val in zip(subs, hs, gates_b, gates_c, vals):
            z = gate_c * val
            prev1 = tail[SUBLANES - 1:SUBLANES, :]
            prev2 = tail[SUBLANES - 2:SUBLANES - 1, :]
            z1 = jnp.where(row == 0, prev1, pltpu.roll(z, 1, 0))
            z2 = jnp.where(row == 0, prev2,
                           jnp.where(row == 1, prev1, pltpu.roll(z, 2, 0)))
            conv = cw[0:1, :] * z2 + cw[1:2, :] * z1 + cw[2:3, :] * z
            tail = z[sub_rows - SUBLANES:sub_rows, :]
            y = (gate_b * conv).astype(BF16)
            o_ref[rs, :] = h + _dot(y, w_out_ref[...])
        carry_ref[...] = tail


def _conv_mixer(h, norm_g, w_in, conv_w, w_out, layer, *, seq_len, tm):
    t, d = h.shape
    kernel = functools.partial(_conv_mixer_kernel, tiles_per_seq=seq_len // tm,
                               sub_rows=FFN_SUB_ROWS)
    return pl.pallas_call(
        kernel,
        grid=(WEIGHT_CAST_STEPS + t // tm,),
        in_specs=[_tile_after_cast(tm, d), _resident((1, d)), _weight_chunk(w_in, layer),
                  _resident(conv_w.shape), _weight_chunk(w_out, layer)],
        out_specs=_tile_after_cast(tm, d),
        out_shape=jax.ShapeDtypeStruct((t, d), F32),
        scratch_shapes=[_bf16_scratch(w_in), _bf16_scratch(w_out),
                        pltpu.VMEM((SUBLANES, d), F32)],
        compiler_params=_params(("arbitrary",)),
        name="conv_mixer",
    )(h, norm_g, w_in, conv_w, w_out)


def _ffn_kernel(*refs, has_attn, has_final, chunks, sub_rows):
    refs = list(refs)
    h_ref = refs.pop(0)
    if has_attn:
        a_ref, wo_f32_ref = refs.pop(0), refs.pop(0)
    g_ref, w13_f32_ref, w2_f32_ref = refs.pop(0), refs.pop(0), refs.pop(0)
    if has_final:
        fg_ref = refs.pop(0)
    o_ref = refs.pop(0)
    if has_attn:
        wo_ref = refs.pop(0)
    w13_ref, w2_ref = refs
    d_ff = w2_ref.shape[0]
    step = pl.program_id(0)

    @pl.when(step < WEIGHT_CAST_STEPS)
    def _():
        _cast_chunk(step, w13_f32_ref, w13_ref)
        _cast_chunk(step, w2_f32_ref, w2_ref)
        if has_attn:
            _cast_chunk(step, wo_f32_ref, wo_ref)

    @pl.when(step >= WEIGHT_CAST_STEPS)
    def _():
        subs = [slice(r, r + sub_rows) for r in range(0, h_ref.shape[0], sub_rows)]
        accs, hns = [], []
        for rs in subs:
            h = h_ref[rs, :]
            if has_attn:
                h = h + _dot(a_ref[rs, :], wo_ref[...])
            accs.append(h)
            hns.append(_rms_norm(h, g_ref[...]).astype(BF16))
        for c0, c1 in chunks:
            gates = [_dot(hn, w13_ref[:, c0:c1]) for hn in hns]
            ups = [_dot(hn, w13_ref[:, d_ff + c0:d_ff + c1]) for hn in hns]
            for i, (gate, up) in enumerate(zip(gates, ups)):
                act = gate * (1.0 / (1.0 + jnp.exp(-gate))) * up
                accs[i] = accs[i] + _dot(act.astype(BF16), w2_ref[c0:c1, :])
        for rs, acc in zip(subs, accs):
            if has_final:
                acc = _rms_norm(acc, fg_ref[...])
            o_ref[rs, :] = acc


def _ffn(h, norm_g, w13, w2, layer, *, tm, attn=None, w_o=None, final_g=None):
    t, d = h.shape
    d_ff = w2.shape[1]
    chunks = tuple((c, min(c + FFN_CHUNK, d_ff)) for c in range(0, d_ff, FFN_CHUNK))
    has_attn, has_final = attn is not None, final_g is not None
    args, specs, scratch = [h], [_tile_after_cast(tm, d)], []
    if has_attn:
        args += [attn, w_o]
        specs += [_tile_after_cast(tm, attn.shape[1]), _weight_chunk(w_o, layer)]
        scratch.append(_bf16_scratch(w_o))
    args += [norm_g, w13, w2]
    specs += [_resident((1, d)), _weight_chunk(w13, layer), _weight_chunk(w2, layer)]
    scratch += [_bf16_scratch(w13), _bf16_scratch(w2)]
    if has_final:
        args.append(final_g)
        specs.append(_resident((1, d)))
    kernel = functools.partial(_ffn_kernel, has_attn=has_attn, has_final=has_final,
                               chunks=chunks, sub_rows=FFN_SUB_ROWS)
    return pl.pallas_call(
        kernel,
        grid=(WEIGHT_CAST_STEPS + t // tm,),
        in_specs=specs,
        out_specs=_tile_after_cast(tm, d),
        out_shape=jax.ShapeDtypeStruct((t, d), F32),
        scratch_shapes=scratch,
        compiler_params=_params(("arbitrary",)),
        name="ffn",
    )(*args)


def _rope_block(x, cs):
    t = x * cs
    return t + pltpu.roll(t, LANES // 2, 1)


def _store_q(q_ref, rs, q, cs, q_scale):
    for hd in range(N_HEADS):
        lo = hd * HEAD_PAD
        q_ref[rs, lo:lo + QK_NOPE_DIM] = (
            q[:, lo:lo + QK_NOPE_DIM] * q_scale).astype(BF16)
        pe = _rope_block(q[:, lo + QK_NOPE_DIM:lo + HEAD_PAD], cs)
        q_ref[rs, lo + QK_NOPE_DIM:lo + HEAD_PAD] = (pe * q_scale).astype(BF16)


def _kv_q_kernel(*refs, kv_lora, q_scale, with_kv, sub_rows):
    if with_kv:
        (h_ref, cs_ref, qn_g_ref, w_dq_ref, qg_ref, w_uq_ref, kv_g_ref, w_dkv_ref,
         lat_g_ref, w_uk_ref, w_uvt_ref, q_ref, k_ref, vt_ref) = refs
    else:
        h_ref, cs_ref, qn_g_ref, w_dq_ref, qg_ref, w_uq_ref, q_ref = refs
    subs = [slice(r, r + sub_rows) for r in range(0, h_ref.shape[0], sub_rows)]
    css = [cs_ref[:, rs].T for rs in subs]

    hns_q, hns_kv = [], []
    for rs in subs:
        h = h_ref[rs, :]
        unit = h * lax.rsqrt(jnp.mean(h * h, axis=-1, keepdims=True) + RMS_EPS)
        hns_q.append((unit * qn_g_ref[...]).astype(BF16))
        if with_kv:
            hns_kv.append((unit * kv_g_ref[...]).astype(BF16))

    cqs = [_dot(hn, w_dq_ref[...]) for hn in hns_q]
    ckvs = [_dot(hn, w_dkv_ref[...]) for hn in hns_kv]
    cqns = [_rms_norm(cq, qg_ref[...]).astype(BF16) for cq in cqs]
    qs = [_dot(cqn, w_uq_ref[...]) for cqn in cqns]
    if with_kv:
        c_lats = [_rms_norm(ckv[:, :kv_lora], lat_g_ref[...]).astype(BF16)
                  for ckv in ckvs]
        k_nopes = [_dot(c_lat, w_uk_ref[...]) for c_lat in c_lats]
        vts = [lax.dot_general(w_uvt_ref[...], c_lat, (((1,), (1,)), ((), ())),
                               preferred_element_type=F32) for c_lat in c_lats]
    for i, rs in enumerate(subs):
        _store_q(q_ref, rs, qs[i], css[i], q_scale)
    if with_kv:
        lane = lax.broadcasted_iota(jnp.int32, (sub_rows, LANES), 1)
        for i, rs in enumerate(subs):
            pe = _rope_block(ckvs[i][:, kv_lora:], css[i])
            k_pe = jnp.where(lane < QK_ROPE_DIM, pe, 0.0).astype(BF16)
            for hd in range(N_HEADS):
                k_ref[rs, hd * HEAD_PAD:hd * HEAD_PAD + QK_NOPE_DIM] = (
                    k_nopes[i][:, hd * QK_NOPE_DIM:(hd + 1) * QK_NOPE_DIM].astype(BF16))
                k_ref[rs, hd * HEAD_PAD + QK_NOPE_DIM:(hd + 1) * HEAD_PAD] = k_pe
            ones = jnp.ones((VT_ROWS - V_HEAD_DIM, sub_rows), BF16)
            for hd in range(N_HEADS):
                vt_ref[hd * VT_ROWS:hd * VT_ROWS + V_HEAD_DIM, rs] = (
                    vts[i][hd * V_HEAD_DIM:(hd + 1) * V_HEAD_DIM, :].astype(BF16))
                vt_ref[hd * VT_ROWS + V_HEAD_DIM:(hd + 1) * VT_ROWS, rs] = ones


def _kv_q(h, cs, q_weights, kv_weights, *, tm, q_scale):
    t, d = h.shape
    with_kv = kv_weights is not None
    args = [h, cs, *q_weights]
    specs = [_row_tile(tm, d), _rope_tile(tm)] + [_resident(w.shape) for w in q_weights]
    out_specs = [_row_tile(tm, N_HEADS * HEAD_PAD)]
    out_shape = [jax.ShapeDtypeStruct((t, N_HEADS * HEAD_PAD), BF16)]
    kv_lora = 0
    if with_kv:
        kv_lora = kv_weights[2].shape[1]
        args += list(kv_weights)
        specs += [_resident(w.shape) for w in kv_weights]
        out_specs += [_row_tile(tm, N_HEADS * HEAD_PAD),
                      pl.BlockSpec((N_HEADS * VT_ROWS, tm), lambda i: (0, i))]
        out_shape += [jax.ShapeDtypeStruct((t, N_HEADS * HEAD_PAD), BF16),
                      jax.ShapeDtypeStruct((N_HEADS * VT_ROWS, t), BF16)]
    kernel = functools.partial(_kv_q_kernel, kv_lora=kv_lora, q_scale=q_scale,
                               with_kv=with_kv, sub_rows=FFN_SUB_ROWS)
    out = pl.pallas_call(
        kernel,
        grid=(t // tm,),
        in_specs=specs,
        out_specs=out_specs,
        out_shape=out_shape,
        compiler_params=_params(("parallel",)),
        name="kv_q" if with_kv else "q_proj",
    )(*args)
    return out if with_kv else out[0]


def _attention_kernel(q_ref, k_ref, vt_ref, o_ref, k_sc, vt_sc, *, heads, tq):
    seq = q_ref.shape[0]
    key = lax.broadcasted_iota(jnp.int32, (tq, tq), 0)
    qry = lax.broadcasted_iota(jnp.int32, (tq, tq), 1)
    causal = key <= qry
    neg = jnp.finfo(F32).min

    def scores(hd, off):
        qc = slice(hd * HEAD_PAD, (hd + 1) * HEAD_PAD)
        k_sc[hd, off:off + tq, :] = k_ref[off:off + tq, qc]
        q = q_ref[off:off + tq, qc]
        k = k_sc[hd, 0:off + tq, :]
        return lax.dot_general(k, q, (((1,), (1,)), ((), ())),
                               preferred_element_type=F32)

    def finish(hd, off, st):
        vc = slice(hd * V_HEAD_DIM, (hd + 1) * V_HEAD_DIM)
        s_diag = jnp.where(causal, st[off:, :], neg)
        m = jnp.max(s_diag, axis=0, keepdims=True)
        if off:
            s_past = st[:off, :]
            m = jnp.maximum(m, jnp.max(s_past, axis=0, keepdims=True))
            pt = jnp.concatenate([jnp.exp2(s_past - m).astype(BF16),
                                  jnp.exp2(s_diag - m).astype(BF16)], axis=0)
        else:
            pt = jnp.exp2(s_diag - m).astype(BF16)
        vr = slice(hd * VT_ROWS, (hd + 1) * VT_ROWS)
        vt_sc[vr, off:off + tq] = vt_ref[vr, off:off + tq]
        out_t = _dot(vt_sc[vr, 0:off + tq], pt)
        denom = out_t[V_HEAD_DIM:V_HEAD_DIM + 1, :]
        o_ref[off:off + tq, vc] = (out_t[:V_HEAD_DIM, :] / denom).T.astype(BF16)

    tiles = [(hd, off) for off in range(0, seq, tq) for hd in range(heads)]
    pending = [scores(*tile) for tile in tiles[:ATTN_LOOKAHEAD]]
    for n, tile in enumerate(tiles):
        if n + ATTN_LOOKAHEAD < len(tiles):
            pending.append(scores(*tiles[n + ATTN_LOOKAHEAD]))
        finish(*tile, pending.pop(0))


def _attention(q, k, vt, *, batch, seq_len):
    hp = ATTN_HEADS_PER_STEP
    kernel = functools.partial(_attention_kernel, heads=hp, tq=ATTN_Q_TILE)
    qk_spec = pl.BlockSpec((seq_len, hp * HEAD_PAD), lambda b, g: (b, g))
    vt_spec = pl.BlockSpec((hp * VT_ROWS, seq_len), lambda b, g: (g, b))
    out_spec = pl.BlockSpec((seq_len, hp * V_HEAD_DIM), lambda b, g: (b, g))
    return pl.pallas_call(
        kernel,
        grid=(batch, N_HEADS // hp),
        in_specs=[qk_spec, qk_spec, vt_spec],
        out_specs=out_spec,
        out_shape=jax.ShapeDtypeStruct((batch * seq_len, N_HEADS * V_HEAD_DIM), BF16),
        scratch_shapes=[pltpu.VMEM((hp, seq_len, HEAD_PAD), BF16),
                        pltpu.VMEM((hp * VT_ROWS, seq_len), BF16)],
        compiler_params=_params(("parallel", "parallel")),
        name="attention",
    )(q, k, vt)


def _rope_columns(w_rope):
    half = QK_ROPE_DIM // 2
    x1, x2 = w_rope[..., :half], w_rope[..., half:]
    return jnp.concatenate([x1, x2, -x2, x1], axis=-1)


def _pad_q_weight(w_uq):
    q_lora = w_uq.shape[0]
    w = w_uq.reshape(q_lora, N_HEADS, QK_NOPE_DIM + QK_ROPE_DIM)
    w = jnp.concatenate([w[..., :QK_NOPE_DIM], _rope_columns(w[..., QK_NOPE_DIM:])], axis=-1)
    return w.reshape(q_lora, N_HEADS * HEAD_PAD)


def kernel(x, positions, conv_norm_g, conv_w_in, conv_w, conv_w_out, conv_ffn_norm_g, conv_ffn_w13, conv_ffn_w2, kv_norm_g, w_dkv, kv_latent_norm_g, w_ukv, mla_norm_g, mla_w_dq, mla_q_norm_g, mla_w_uq, mla_w_o, mla_ffn_norm_g, mla_ffn_w13, mla_ffn_w2, final_norm_g):
    batch, seq_len, d = x.shape
    t = batch * seq_len
    tm = TOKEN_TILE
    n_conv, n_mla = conv_w_in.shape[0], mla_w_dq.shape[0]
    kv_lora = kv_latent_norm_g.shape[0]
    assert seq_len % tm == 0 and seq_len % ATTN_Q_TILE == 0

    inv_freq = ROPE_THETA ** (-jnp.arange(0, QK_ROPE_DIM, 2, dtype=F32) / QK_ROPE_DIM)
    ang = positions.astype(F32).reshape(1, t) * inv_freq.reshape(-1, 1)
    cos, sin = jnp.cos(ang), jnp.sin(ang)
    cs = jnp.concatenate([cos, cos, sin, sin], axis=0)

    row = lambda g: g.reshape(1, -1)
    h = x.reshape(t, d)

    for i in range(n_conv):
        h = _conv_mixer(h, row(conv_norm_g[i]), conv_w_in, conv_w[i], conv_w_out, i,
                        seq_len=seq_len, tm=tm)
        h = _ffn(h, row(conv_ffn_norm_g[i]), conv_ffn_w13, conv_ffn_w2, i, tm=tm)

    w_dkv_p = jnp.concatenate([w_dkv[:, :kv_lora], _rope_columns(w_dkv[:, kv_lora:])], axis=-1)
    w_ukv_h = w_ukv.reshape(kv_lora, N_HEADS, QK_NOPE_DIM + V_HEAD_DIM)
    w_uk = w_ukv_h[..., :QK_NOPE_DIM].reshape(kv_lora, N_HEADS * QK_NOPE_DIM)
    w_uvt = w_ukv_h[..., QK_NOPE_DIM:].reshape(kv_lora, N_HEADS * V_HEAD_DIM).T
    kv_weights = (row(kv_norm_g), w_dkv_p.astype(BF16), row(kv_latent_norm_g),
                  w_uk.astype(BF16), w_uvt.astype(BF16))

    q_scale = float((QK_NOPE_DIM + QK_ROPE_DIM) ** -0.5 * 1.4426950408889634)
    for j in range(n_mla):
        q_weights = (row(mla_norm_g[j]), mla_w_dq[j].astype(BF16),
                     row(mla_q_norm_g[j]), _pad_q_weight(mla_w_uq[j]).astype(BF16))
        if j == 0:
            q, k, vt = _kv_q(h, cs, q_weights, kv_weights, tm=tm, q_scale=q_scale)
        else:
            q = _kv_q(h, cs, q_weights, None, tm=tm, q_scale=q_scale)
        attn = _attention(q, k, vt, batch=batch, seq_len=seq_len)
        last = j == n_mla - 1
        h = _ffn(h, row(mla_ffn_norm_g[j]), mla_ffn_w13, mla_ffn_w2, j, tm=tm,
                 attn=attn, w_o=mla_w_o,
                 final_g=row(final_norm_g) if last else None)
    return h.reshape(batch, seq_len, d)
```

```python
import functools

import jax
import jax.numpy as jnp
from jax import lax
from jax.experimental import pallas as pl
from jax.experimental.pallas import tpu as pltpu

N_HEADS = 8
QK_NOPE_DIM = 128
QK_ROPE_DIM = 64
V_HEAD_DIM = 128
CONV_WIDTH = 3
ROPE_THETA = 10000.0
RMS_EPS = 1e-6

LANES = 128
SUBLANES = 8
HEAD_PAD = 2 * LANES
BF16_ROWS = 16
VT_ROWS = V_HEAD_DIM + BF16_ROWS
VMEM_LIMIT_BYTES = 58 * 1024 * 1024

TOKEN_TILE = 512
CONV_TOKEN_TILE = 1024
ATTN_Q_TILE = 256
ATTN_HEADS_PER_STEP = 4
ATTN_LOOKAHEAD = 3
FFN_CHUNK = 1024
FFN_SUB_ROWS = 256
WEIGHT_CAST_STEPS = 16

BF16 = jnp.bfloat16
F32 = jnp.float32


def _dot(a, b):
    return jnp.dot(a, b, preferred_element_type=F32)


def _rms_norm(x, g):
    ms = jnp.mean(x * x, axis=-1, keepdims=True)
    return x * lax.rsqrt(ms + RMS_EPS) * g


def _resident(shape):
    return pl.BlockSpec(shape, lambda *_: (0,) * len(shape),
                        pipeline_mode=pl.Buffered(1))


def _row_tile(tm, width):
    return pl.BlockSpec((tm, width), lambda i: (i, 0))


def _rope_tile(tm):
    return pl.BlockSpec((LANES, tm), lambda i: (0, i))


def _params(semantics):
    return pltpu.CompilerParams(dimension_semantics=semantics,
                                vmem_limit_bytes=VMEM_LIMIT_BYTES)


def _weight_chunk(w_stacked, layer):
    _, rows, cols = w_stacked.shape
    last = WEIGHT_CAST_STEPS - 1
    return pl.BlockSpec((None, rows // WEIGHT_CAST_STEPS, cols),
                        lambda i: (layer, jnp.minimum(i, last), 0))


def _tile_after_cast(tm, width):
    return pl.BlockSpec((tm, width),
                        lambda i: (jnp.maximum(i - WEIGHT_CAST_STEPS, 0), 0))


def _cast_chunk(step, src_ref, dst_ref):
    rows = src_ref.shape[0]
    start = pl.multiple_of(step * rows, rows)
    dst_ref[pl.ds(start, rows), :] = src_ref[...].astype(BF16)


def _bf16_scratch(w_stacked):
    return pltpu.VMEM(w_stacked.shape[1:], BF16)


def _conv_mixer_kernel(h_ref, g_ref, w_in_f32_ref, cw_ref, w_out_f32_ref, o_ref,
                       w_in_ref, w_out_ref, carry_ref, *, tiles_per_seq, sub_rows):
    d = h_ref.shape[1]
    tm = h_ref.shape[0]
    step = pl.program_id(0)

    @pl.when(step < WEIGHT_CAST_STEPS)
    def _():
        _cast_chunk(step, w_in_f32_ref, w_in_ref)
        _cast_chunk(step, w_out_f32_ref, w_out_ref)

    @pl.when(step >= WEIGHT_CAST_STEPS)
    def _():
        @pl.when((step - WEIGHT_CAST_STEPS) % tiles_per_seq == 0)
        def _():
            carry_ref[...] = jnp.zeros_like(carry_ref)

        subs = [slice(r, r + sub_rows) for r in range(0, tm, sub_rows)]
        hs = [h_ref[rs, :] for rs in subs]
        hns = [_rms_norm(h, g_ref[...]).astype(BF16) for h in hs]
        gates_c = [_dot(hn, w_in_ref[:, d:2 * d]) for hn in hns]
        vals = [_dot(hn, w_in_ref[:, 2 * d:3 * d]) for hn in hns]
        gates_b = [_dot(hn, w_in_ref[:, 0:d]) for hn in hns]

        row = lax.broadcasted_iota(jnp.int32, (sub_rows, d), 0)
        cw = cw_ref[...]
        tail = carry_ref[...]
        for rs, h, gate_b, gate_c, val in zip(subs, hs, gates_b, gates_c, vals):
            z = gate_c * val
            prev1 = tail[SUBLANES - 1:SUBLANES, :]
            prev2 = tail[SUBLANES - 2:SUBLANES - 1, :]
            z1 = jnp.where(row == 0, prev1, pltpu.roll(z, 1, 0))
            z2 = jnp.where(row == 0, prev2,
                           jnp.where(row == 1, prev1, pltpu.roll(z, 2, 0)))
            conv = cw[0:1, :] * z2 + cw[1:2, :] * z1 + cw[2:3, :] * z
            tail = z[sub_rows - SUBLANES:sub_rows, :]
            y = (gate_b * conv).astype(BF16)
            o_ref[rs, :] = h + _dot(y, w_out_ref[...])
        carry_ref[...] = tail


def _conv_mixer(h, norm_g, w_in, conv_w, w_out, layer, *, seq_len, tm):
    t, d = h.shape
    kernel = functools.partial(_conv_mixer_kernel, tiles_per_seq=seq_len // tm,
                               sub_rows=FFN_SUB_ROWS)
    return pl.pallas_call(
        kernel,
        grid=(WEIGHT_CAST_STEPS + t // tm,),
        in_specs=[_tile_after_cast(tm, d), _resident((1, d)), _weight_chunk(w_in, layer),
                  _resident(conv_w.shape), _weight_chunk(w_out, layer)],
        out_specs=_tile_after_cast(tm, d),
        out_shape=jax.ShapeDtypeStruct((t, d), F32),
        scratch_shapes=[_bf16_scratch(w_in), _bf16_scratch(w_out),
                        pltpu.VMEM((SUBLANES, d), F32)],
        compiler_params=_params(("arbitrary",)),
        name="conv_mixer",
    )(h, norm_g, w_in, conv_w, w_out)


def _ffn_kernel(*refs, has_attn, has_final, chunks, sub_rows):
    refs = list(refs)
    h_ref = refs.pop(0)
    if has_attn:
        a_ref, wo_f32_ref = refs.pop(0), refs.pop(0)
    g_ref, w13_f32_ref, w2_f32_ref = refs.pop(0), refs.pop(0), refs.pop(0)
    if has_final:
        fg_ref = refs.pop(0)
    o_ref = refs.pop(0)
    if has_attn:
        wo_ref = refs.pop(0)
    w13_ref, w2_ref = refs
    d_ff = w2_ref.shape[0]
    step = pl.program_id(0)

    @pl.when(step < WEIGHT_CAST_STEPS)
    def _():
        _cast_chunk(step, w13_f32_ref, w13_ref)
        _cast_chunk(step, w2_f32_ref, w2_ref)
        if has_attn:
            _cast_chunk(step, wo_f32_ref, wo_ref)

    @pl.when(step >= WEIGHT_CAST_STEPS)
    def _():
        subs = [slice(r, r + sub_rows) for r in range(0, h_ref.shape[0], sub_rows)]
        accs, hns = [], []
        for rs in subs:
            h = h_ref[rs, :]
            if has_attn:
                h = h + _dot(a_ref[rs, :], wo_ref[...])
            accs.append(h)
            hns.append(_rms_norm(h, g_ref[...]).astype(BF16))
        for c0, c1 in chunks:
            gates = [_dot(hn, w13_ref[:, c0:c1]) for hn in hns]
            ups = [_dot(hn, w13_ref[:, d_ff + c0:d_ff + c1]) for hn in hns]
            for i, (gate, up) in enumerate(zip(gates, ups)):
                act = gate * (1.0 / (1.0 + jnp.exp(-gate))) * up
                accs[i] = accs[i] + _dot(act.astype(BF16), w2_ref[c0:c1, :])
        for rs, acc in zip(subs, accs):
            if has_final:
                acc = _rms_norm(acc, fg_ref[...])
            o_ref[rs, :] = acc


def _ffn(h, norm_g, w13, w2, layer, *, tm, attn=None, w_o=None, final_g=None):
    t, d = h.shape
    d_ff = w2.shape[1]
    chunks = tuple((c, min(c + FFN_CHUNK, d_ff)) for c in range(0, d_ff, FFN_CHUNK))
    has_attn, has_final = attn is not None, final_g is not None
    args, specs, scratch = [h], [_tile_after_cast(tm, d)], []
    if has_attn:
        args += [attn, w_o]
        specs += [_tile_after_cast(tm, attn.shape[1]), _weight_chunk(w_o, layer)]
        scratch.append(_bf16_scratch(w_o))
    args += [norm_g, w13, w2]
    specs += [_resident((1, d)), _weight_chunk(w13, layer), _weight_chunk(w2, layer)]
    scratch += [_bf16_scratch(w13), _bf16_scratch(w2)]
    if has_final:
        args.append(final_g)
        specs.append(_resident((1, d)))
    kernel = functools.partial(_ffn_kernel, has_attn=has_attn, has_final=has_final,
                               chunks=chunks, sub_rows=FFN_SUB_ROWS)
    return pl.pallas_call(
        kernel,
        grid=(WEIGHT_CAST_STEPS + t // tm,),
        in_specs=specs,
        out_specs=_tile_after_cast(tm, d),
        out_shape=jax.ShapeDtypeStruct((t, d), F32),
        scratch_shapes=scratch,
        compiler_params=_params(("arbitrary",)),
        name="ffn",
    )(*args)


def _rope_block(x, cs):
    t = x * cs
    return t + pltpu.roll(t, LANES // 2, 1)


def _store_q(q_ref, rs, q, cs, q_scale):
    for hd in range(N_HEADS):
        lo = hd * HEAD_PAD
        q_ref[rs, lo:lo + QK_NOPE_DIM] = (
            q[:, lo:lo + QK_NOPE_DIM] * q_scale).astype(BF16)
        pe = _rope_block(q[:, lo + QK_NOPE_DIM:lo + HEAD_PAD], cs)
        q_ref[rs, lo + QK_NOPE_DIM:lo + HEAD_PAD] = (pe * q_scale).astype(BF16)


def _kv_q_kernel(*refs, kv_lora, q_scale, with_kv, sub_rows):
    if with_kv:
        (h_ref, cs_ref, qn_g_ref, w_dq_ref, qg_ref, w_uq_ref, kv_g_ref, w_dkv_ref,
         lat_g_ref, w_uk_ref, w_uvt_ref, q_ref, k_ref, vt_ref) = refs
    else:
        h_ref, cs_ref, qn_g_ref, w_dq_ref, qg_ref, w_uq_ref, q_ref = refs
    subs = [slice(r, r + sub_rows) for r in range(0, h_ref.shape[0], sub_rows)]
    css = [cs_ref[:, rs].T for rs in subs]

    hns_q, hns_kv = [], []
    for rs in subs:
        h = h_ref[rs, :]
        unit = h * lax.rsqrt(jnp.mean(h * h, axis=-1, keepdims=True) + RMS_EPS)
        hns_q.append((unit * qn_g_ref[...]).astype(BF16))
        if with_kv:
            hns_kv.append((unit * kv_g_ref[...]).astype(BF16))

    cqs = [_dot(hn, w_dq_ref[...]) for hn in hns_q]
    ckvs = [_dot(hn, w_dkv_ref[...]) for hn in hns_kv]
    cqns = [_rms_norm(cq, qg_ref[...]).astype(BF16) for cq in cqs]
    qs = [_dot(cqn, w_uq_ref[...]) for cqn in cqns]
    if with_kv:
        c_lats = [_rms_norm(ckv[:, :kv_lora], lat_g_ref[...]).astype(BF16)
                  for ckv in ckvs]
        k_nopes = [_dot(c_lat, w_uk_ref[...]) for c_lat in c_lats]
        vts = [lax.dot_general(w_uvt_ref[...], c_lat, (((1,), (1,)), ((), ())),
                               preferred_element_type=F32) for c_lat in c_lats]
    for i, rs in enumerate(subs):
        _store_q(q_ref, rs, qs[i], css[i], q_scale)
    if with_kv:
        lane = lax.broadcasted_iota(jnp.int32, (sub_rows, LANES), 1)
        for i, rs in enumerate(subs):
            pe = _rope_block(ckvs[i][:, kv_lora:], css[i])
            k_pe = jnp.where(lane < QK_ROPE_DIM, pe, 0.0).astype(BF16)
            for hd in range(N_HEADS):
                k_ref[rs, hd * HEAD_PAD:hd * HEAD_PAD + QK_NOPE_DIM] = (
                    k_nopes[i][:, hd * QK_NOPE_DIM:(hd + 1) * QK_NOPE_DIM].astype(BF16))
                k_ref[rs, hd * HEAD_PAD + QK_NOPE_DIM:(hd + 1) * HEAD_PAD] = k_pe
            ones = jnp.ones((VT_ROWS - V_HEAD_DIM, sub_rows), BF16)
            for hd in range(N_HEADS):
                vt_ref[hd * VT_ROWS:hd * VT_ROWS + V_HEAD_DIM, rs] = (
                    vts[i][hd * V_HEAD_DIM:(hd + 1) * V_HEAD_DIM, :].astype(BF16))
                vt_ref[hd * VT_ROWS + V_HEAD_DIM:(hd + 1) * VT_ROWS, rs] = ones


def _kv_q(h, cs, q_weights, kv_weights, *, tm, q_scale):
    t, d = h.shape
    with_kv = kv_weights is not None
    args = [h, cs, *q_weights]
    specs = [_row_tile(tm, d), _rope_tile(tm)] + [_resident(w.shape) for w in q_weights]
    out_specs = [_row_tile(tm, N_HEADS * HEAD_PAD)]
    out_shape = [jax.ShapeDtypeStruct((t, N_HEADS * HEAD_PAD), BF16)]
    kv_lora = 0
    if with_kv:
        kv_lora = kv_weights[2].shape[1]
        args += list(kv_weights)
        specs += [_resident(w.shape) for w in kv_weights]
        out_specs += [_row_tile(tm, N_HEADS * HEAD_PAD),
                      pl.BlockSpec((N_HEADS * VT_ROWS, tm), lambda i: (0, i))]
        out_shape += [jax.ShapeDtypeStruct((t, N_HEADS * HEAD_PAD), BF16),
                      jax.ShapeDtypeStruct((N_HEADS * VT_ROWS, t), BF16)]
    kernel = functools.partial(_kv_q_kernel, kv_lora=kv_lora, q_scale=q_scale,
                               with_kv=with_kv, sub_rows=FFN_SUB_ROWS)
    out = pl.pallas_call(
        kernel,
        grid=(t // tm,),
        in_specs=specs,
        out_specs=out_specs,
        out_shape=out_shape,
        compiler_params=_params(("parallel",)),
        name="kv_q" if with_kv else "q_proj",
    )(*args)
    return out if with_kv else out[0]


def _attention_kernel(q_ref, k_ref, vt_ref, o_ref, k_sc, vt_sc, *, heads, tq):
    seq = q_ref.shape[0]
    key = lax.broadcasted_iota(jnp.int32, (tq, tq), 0)
    qry = lax.broadcasted_iota(jnp.int32, (tq, tq), 1)
    causal = key <= qry
    neg = jnp.finfo(F32).min

    def scores(hd, off):
        qc = slice(hd * HEAD_PAD, (hd + 1) * HEAD_PAD)
        k_sc[hd, off:off + tq, :] = k_ref[off:off + tq, qc]
        q = q_ref[off:off + tq, qc]
        k = k_sc[hd, 0:off + tq, :]
        return lax.dot_general(k, q, (((1,), (1,)), ((), ())),
                               preferred_element_type=F32)

    def finish(hd, off, st):
        vc = slice(hd * V_HEAD_DIM, (hd + 1) * V_HEAD_DIM)
        s_diag = jnp.where(causal, st[off:, :], neg)
        m = jnp.max(s_diag, axis=0, keepdims=True)
        if off:
            s_past = st[:off, :]
            m = jnp.maximum(m, jnp.max(s_past, axis=0, keepdims=True))
            pt = jnp.concatenate([jnp.exp2(s_past - m).astype(BF16),
                                  jnp.exp2(s_diag - m).astype(BF16)], axis=0)
        else:
            pt = jnp.exp2(s_diag - m).astype(BF16)
        vr = slice(hd * VT_ROWS, (hd + 1) * VT_ROWS)
        vt_sc[vr, off:off + tq] = vt_ref[vr, off:off + tq]
        out_t = _dot(vt_sc[vr, 0:off + tq], pt)
        inv_denom = 1.0 / out_t[V_HEAD_DIM:V_HEAD_DIM + 1, :]
        o_ref[off:off + tq, vc] = (out_t[:V_HEAD_DIM, :] * inv_denom).T.astype(BF16)

    tiles = [(hd, off) for off in range(0, seq, tq) for hd in range(heads)]
    pending = [scores(*tile) for tile in tiles[:ATTN_LOOKAHEAD]]
    for n, tile in enumerate(tiles):
        if n + ATTN_LOOKAHEAD < len(tiles):
            pending.append(scores(*tiles[n + ATTN_LOOKAHEAD]))
        finish(*tile, pending.pop(0))


def _attention(q, k, vt, *, batch, seq_len):
    hp = ATTN_HEADS_PER_STEP
    kernel = functools.partial(_attention_kernel, heads=hp, tq=ATTN_Q_TILE)
    qk_spec = pl.BlockSpec((seq_len, hp * HEAD_PAD), lambda b, g: (b, g))
    vt_spec = pl.BlockSpec((hp * VT_ROWS, seq_len), lambda b, g: (g, b))
    out_spec = pl.BlockSpec((seq_len, hp * V_HEAD_DIM), lambda b, g: (b, g))
    return pl.pallas_call(
        kernel,
        grid=(batch, N_HEADS // hp),
        in_specs=[qk_spec, qk_spec, vt_spec],
        out_specs=out_spec,
        out_shape=jax.ShapeDtypeStruct((batch * seq_len, N_HEADS * V_HEAD_DIM), BF16),
        scratch_shapes=[pltpu.VMEM((hp, seq_len, HEAD_PAD), BF16),
                        pltpu.VMEM((hp * VT_ROWS, seq_len), BF16)],
        compiler_params=_params(("parallel", "parallel")),
        name="attention",
    )(q, k, vt)


def _rope_columns(w_rope):
    half = QK_ROPE_DIM // 2
    x1, x2 = w_rope[..., :half], w_rope[..., half:]
    return jnp.concatenate([x1, x2, -x2, x1], axis=-1)


def _pad_q_weight(w_uq):
    q_lora = w_uq.shape[0]
    w = w_uq.reshape(q_lora, N_HEADS, QK_NOPE_DIM + QK_ROPE_DIM)
    w = jnp.concatenate([w[..., :QK_NOPE_DIM], _rope_columns(w[..., QK_NOPE_DIM:])], axis=-1)
    return w.reshape(q_lora, N_HEADS * HEAD_PAD)


def kernel(x, positions, conv_norm_g, conv_w_in, conv_w, conv_w_out, conv_ffn_norm_g, conv_ffn_w13, conv_ffn_w2, kv_norm_g, w_dkv, kv_latent_norm_g, w_ukv, mla_norm_g, mla_w_dq, mla_q_norm_g, mla_w_uq, mla_w_o, mla_ffn_norm_g, mla_ffn_w13, mla_ffn_w2, final_norm_g):
    batch, seq_len, d = x.shape
    t = batch * seq_len
    tm = TOKEN_TILE
    n_conv, n_mla = conv_w_in.shape[0], mla_w_dq.shape[0]
    kv_lora = kv_latent_norm_g.shape[0]
    assert seq_len % tm == 0 and seq_len % ATTN_Q_TILE == 0

    inv_freq = ROPE_THETA ** (-jnp.arange(0, QK_ROPE_DIM, 2, dtype=F32) / QK_ROPE_DIM)
    ang = positions.astype(F32).reshape(1, t) * inv_freq.reshape(-1, 1)
    cos, sin = jnp.cos(ang), jnp.sin(ang)
    cs = jnp.concatenate([cos, cos, sin, sin], axis=0)

    row = lambda g: g.reshape(1, -1)
    h = x.reshape(t, d)

    for i in range(n_conv):
        h = _conv_mixer(h, row(conv_norm_g[i]), conv_w_in, conv_w[i], conv_w_out, i,
                        seq_len=seq_len, tm=CONV_TOKEN_TILE)
        h = _ffn(h, row(conv_ffn_norm_g[i]), conv_ffn_w13, conv_ffn_w2, i,
                 tm=CONV_TOKEN_TILE)

    w_dkv_p = jnp.concatenate([w_dkv[:, :kv_lora], _rope_columns(w_dkv[:, kv_lora:])], axis=-1)
    w_ukv_h = w_ukv.reshape(kv_lora, N_HEADS, QK_NOPE_DIM + V_HEAD_DIM)
    w_uk = w_ukv_h[..., :QK_NOPE_DIM].reshape(kv_lora, N_HEADS * QK_NOPE_DIM)
    w_uvt = w_ukv_h[..., QK_NOPE_DIM:].reshape(kv_lora, N_HEADS * V_HEAD_DIM).T
    kv_weights = (row(kv_norm_g), w_dkv_p.astype(BF16), row(kv_latent_norm_g),
                  w_uk.astype(BF16), w_uvt.astype(BF16))

    q_scale = float((QK_NOPE_DIM + QK_ROPE_DIM) ** -0.5 * 1.4426950408889634)
    for j in range(n_mla):
        q_weights = (row(mla_norm_g[j]), mla_w_dq[j].astype(BF16),
                     row(mla_q_norm_g[j]), _pad_q_weight(mla_w_uq[j]).astype(BF16))
        if j == 0:
            q, k, vt = _kv_q(h, cs, q_weights, kv_weights, tm=tm, q_scale=q_scale)
        else:
            q = _kv_q(h, cs, q_weights, None, tm=tm, q_scale=q_scale)
        attn = _attention(q, k, vt, batch=batch, seq_len=seq_len)
        last = j == n_mla - 1
        h = _ffn(h, row(mla_ffn_norm_g[j]), mla_ffn_w13, mla_ffn_w2, j, tm=CONV_TOKEN_TILE,
                 attn=attn, w_o=mla_w_o,
                 final_g=row(final_norm_g) if last else None)
    return h.reshape(batch, seq_len, d)
```

```python
import functools

import jax
import jax.numpy as jnp
from jax import lax
from jax.experimental import pallas as pl
from jax.experimental.pallas import tpu as pltpu

N_HEADS = 8
QK_NOPE_DIM = 128
QK_ROPE_DIM = 64
V_HEAD_DIM = 128
CONV_WIDTH = 3
ROPE_THETA = 10000.0
RMS_EPS = 1e-6

LANES = 128
SUBLANES = 8
HEAD_PAD = 2 * LANES
BF16_ROWS = 16
VT_ROWS = V_HEAD_DIM + BF16_ROWS
VMEM_LIMIT_BYTES = 58 * 1024 * 1024

TOKEN_TILE = 1024
ATTN_Q_TILE = 256
ATTN_HEADS_PER_STEP = 4
ATTN_LOOKAHEAD = 3
FFN_CHUNK = 1024
FFN_SUB_ROWS = 256
WEIGHT_CAST_STEPS = 16

BF16 = jnp.bfloat16
F32 = jnp.float32


def _dot(a, b):
    return jnp.dot(a, b, preferred_element_type=F32)


def _rms_norm(x, g):
    ms = jnp.mean(x * x, axis=-1, keepdims=True)
    return x * lax.rsqrt(ms + RMS_EPS) * g


def _resident(shape):
    return pl.BlockSpec(shape, lambda *_: (0,) * len(shape),
                        pipeline_mode=pl.Buffered(1))


def _row_tile(tm, width):
    return pl.BlockSpec((tm, width), lambda i: (i, 0))


def _rope_tile(tm):
    return pl.BlockSpec((LANES, tm), lambda i: (0, i))


def _params(semantics):
    return pltpu.CompilerParams(dimension_semantics=semantics,
                                vmem_limit_bytes=VMEM_LIMIT_BYTES)


def _weight_chunk(w_stacked, layer):
    _, rows, cols = w_stacked.shape
    last = WEIGHT_CAST_STEPS - 1
    return pl.BlockSpec((None, rows // WEIGHT_CAST_STEPS, cols),
                        lambda i: (layer, jnp.minimum(i, last), 0))


def _tile_after_cast(tm, width):
    return pl.BlockSpec((tm, width),
                        lambda i: (jnp.maximum(i - WEIGHT_CAST_STEPS, 0), 0))


def _cast_chunk(step, src_ref, dst_ref):
    rows = src_ref.shape[0]
    start = pl.multiple_of(step * rows, rows)
    dst_ref[pl.ds(start, rows), :] = src_ref[...].astype(BF16)


def _bf16_scratch(w_stacked):
    return pltpu.VMEM(w_stacked.shape[1:], BF16)


def _conv_mixer_kernel(h_ref, g_ref, w_in_f32_ref, cw_ref, w_out_f32_ref, o_ref,
                       w_in_ref, w_out_ref, carry_ref, *, tiles_per_seq, sub_rows):
    d = h_ref.shape[1]
    tm = h_ref.shape[0]
    step = pl.program_id(0)

    @pl.when(step < WEIGHT_CAST_STEPS)
    def _():
        _cast_chunk(step, w_in_f32_ref, w_in_ref)
        _cast_chunk(step, w_out_f32_ref, w_out_ref)

    @pl.when(step >= WEIGHT_CAST_STEPS)
    def _():
        @pl.when((step - WEIGHT_CAST_STEPS) % tiles_per_seq == 0)
        def _():
            carry_ref[...] = jnp.zeros_like(carry_ref)

        subs = [slice(r, r + sub_rows) for r in range(0, tm, sub_rows)]
        hs = [h_ref[rs, :] for rs in subs]
        hns = [_rms_norm(h, g_ref[...]).astype(BF16) for h in hs]
        gates_c = [_dot(hn, w_in_ref[:, d:2 * d]) for hn in hns]
        vals = [_dot(hn, w_in_ref[:, 2 * d:3 * d]) for hn in hns]
        gates_b = [_dot(hn, w_in_ref[:, 0:d]) for hn in hns]

        row = lax.broadcasted_iota(jnp.int32, (sub_rows, d), 0)
        cw = cw_ref[...]
        tail = carry_ref[...]
        for rs, h, gate_b, gate_c, val in zip(subs, hs, gates_b, gates_c, vals):
            z = gate_c * val
            prev1 = tail[SUBLANES - 1:SUBLANES, :]
            prev2 = tail[SUBLANES - 2:SUBLANES - 1, :]
            z1 = jnp.where(row == 0, prev1, pltpu.roll(z, 1, 0))
            z2 = jnp.where(row == 0, prev2,
                           jnp.where(row == 1, prev1, pltpu.roll(z, 2, 0)))
            conv = cw[0:1, :] * z2 + cw[1:2, :] * z1 + cw[2:3, :] * z
            tail = z[sub_rows - SUBLANES:sub_rows, :]
            y = (gate_b * conv).astype(BF16)
            o_ref[rs, :] = h + _dot(y, w_out_ref[...])
        carry_ref[...] = tail


def _conv_mixer(h, norm_g, w_in, conv_w, w_out, layer, *, seq_len, tm):
    t, d = h.shape
    kernel = functools.partial(_conv_mixer_kernel, tiles_per_seq=seq_len // tm,
                               sub_rows=FFN_SUB_ROWS)
    return pl.pallas_call(
        kernel,
        grid=(WEIGHT_CAST_STEPS + t // tm,),
        in_specs=[_tile_after_cast(tm, d), _resident((1, d)), _weight_chunk(w_in, layer),
                  _resident(conv_w.shape), _weight_chunk(w_out, layer)],
        out_specs=_tile_after_cast(tm, d),
        out_shape=jax.ShapeDtypeStruct((t, d), F32),
        scratch_shapes=[_bf16_scratch(w_in), _bf16_scratch(w_out),
                        pltpu.VMEM((SUBLANES, d), F32)],
        compiler_params=_params(("arbitrary",)),
        name="conv_mixer",
    )(h, norm_g, w_in, conv_w, w_out)


def _ffn_kernel(*refs, has_attn, has_final, chunks, sub_rows):
    refs = list(refs)
    h_ref = refs.pop(0)
    if has_attn:
        a_ref, wo_f32_ref = refs.pop(0), refs.pop(0)
    g_ref, w13_f32_ref, w2_f32_ref = refs.pop(0), refs.pop(0), refs.pop(0)
    if has_final:
        fg_ref = refs.pop(0)
    o_ref = refs.pop(0)
    if has_attn:
        wo_ref = refs.pop(0)
    w13_ref, w2_ref = refs
    d_ff = w2_ref.shape[0]
    step = pl.program_id(0)

    @pl.when(step < WEIGHT_CAST_STEPS)
    def _():
        _cast_chunk(step, w13_f32_ref, w13_ref)
        _cast_chunk(step, w2_f32_ref, w2_ref)
        if has_attn:
            _cast_chunk(step, wo_f32_ref, wo_ref)

    @pl.when(step >= WEIGHT_CAST_STEPS)
    def _():
        subs = [slice(r, r + sub_rows) for r in range(0, h_ref.shape[0], sub_rows)]
        accs, hns = [], []
        for rs in subs:
            h = h_ref[rs, :]
            if has_attn:
                h = h + _dot(a_ref[rs, :], wo_ref[...])
            accs.append(h)
            hns.append(_rms_norm(h, g_ref[...]).astype(BF16))
        for c0, c1 in chunks:
            gates = [_dot(hn, w13_ref[:, c0:c1]) for hn in hns]
            ups = [_dot(hn, w13_ref[:, d_ff + c0:d_ff + c1]) for hn in hns]
            for i, (gate, up) in enumerate(zip(gates, ups)):
                act = gate * (1.0 / (1.0 + jnp.exp(-gate))) * up
                accs[i] = accs[i] + _dot(act.astype(BF16), w2_ref[c0:c1, :])
        for rs, acc in zip(subs, accs):
            if has_final:
                acc = _rms_norm(acc, fg_ref[...])
            o_ref[rs, :] = acc


def _ffn(h, norm_g, w13, w2, layer, *, tm, attn=None, w_o=None, final_g=None):
    t, d = h.shape
    d_ff = w2.shape[1]
    chunks = tuple((c, min(c + FFN_CHUNK, d_ff)) for c in range(0, d_ff, FFN_CHUNK))
    has_attn, has_final = attn is not None, final_g is not None
    args, specs, scratch = [h], [_tile_after_cast(tm, d)], []
    if has_attn:
        args += [attn, w_o]
        specs += [_tile_after_cast(tm, attn.shape[1]), _weight_chunk(w_o, layer)]
        scratch.append(_bf16_scratch(w_o))
    args += [norm_g, w13, w2]
    specs += [_resident((1, d)), _weight_chunk(w13, layer), _weight_chunk(w2, layer)]
    scratch += [_bf16_scratch(w13), _bf16_scratch(w2)]
    if has_final:
        args.append(final_g)
        specs.append(_resident((1, d)))
    kernel = functools.partial(_ffn_kernel, has_attn=has_attn, has_final=has_final,
                               chunks=chunks, sub_rows=FFN_SUB_ROWS)
    return pl.pallas_call(
        kernel,
        grid=(WEIGHT_CAST_STEPS + t // tm,),
        in_specs=specs,
        out_specs=_tile_after_cast(tm, d),
        out_shape=jax.ShapeDtypeStruct((t, d), F32),
        scratch_shapes=scratch,
        compiler_params=_params(("arbitrary",)),
        name="ffn",
    )(*args)


def _rope_block(x, cs):
    t = x * cs
    return t + pltpu.roll(t, LANES // 2, 1)


def _store_q(q_ref, rs, q, cs, q_scale):
    lane = lax.broadcasted_iota(jnp.int32, cs.shape, 1)
    half = LANES // 2
    cs_swapped = pltpu.roll(cs, half, 1)
    cos4 = jnp.where(lane < half, cs, cs_swapped)
    first_half = (lane & (half - 1)) < half // 2
    sin4 = (jnp.where(lane < half, cs_swapped, cs)
            * jnp.where(first_half, -1.0, 1.0))
    nope_width = N_HEADS * QK_NOPE_DIM
    pe_cols = []
    for c in range(N_HEADS * QK_ROPE_DIM // LANES):
        x = q[:, nope_width + c * LANES:nope_width + (c + 1) * LANES]
        swapped = jnp.where(first_half, pltpu.roll(x, LANES - half // 2, 1),
                            pltpu.roll(x, half // 2, 1))
        pe_cols.append((x * cos4 + swapped * sin4) * q_scale)
    for hd in range(N_HEADS):
        lo = hd * HEAD_PAD
        q_ref[rs, lo:lo + QK_NOPE_DIM] = (
            q[:, hd * QK_NOPE_DIM:(hd + 1) * QK_NOPE_DIM] * q_scale).astype(BF16)
        col = pe_cols[hd // 2]
        if hd % 2:
            col = pltpu.roll(col, half, 1)
        q_ref[rs, lo + QK_NOPE_DIM:lo + HEAD_PAD] = col.astype(BF16)


def _kv_q_kernel(*refs, kv_lora, q_scale, with_kv, sub_rows):
    if with_kv:
        (h_ref, cs_ref, qn_g_ref, w_dq_ref, qg_ref, w_uq_ref, kv_g_ref, w_dkv_ref,
         lat_g_ref, w_uk_ref, w_uvt_ref, q_ref, k_nope_ref, k_pe_ref, vt_ref) = refs
    else:
        h_ref, cs_ref, qn_g_ref, w_dq_ref, qg_ref, w_uq_ref, q_ref = refs
    subs = [slice(r, r + sub_rows) for r in range(0, h_ref.shape[0], sub_rows)]
    css = [cs_ref[:, rs].T for rs in subs]

    hns_q, hns_kv = [], []
    for rs in subs:
        h = h_ref[rs, :]
        unit = h * lax.rsqrt(jnp.mean(h * h, axis=-1, keepdims=True) + RMS_EPS)
        hns_q.append((unit * qn_g_ref[...]).astype(BF16))
        if with_kv:
            hns_kv.append((unit * kv_g_ref[...]).astype(BF16))

    cqs = [_dot(hn, w_dq_ref[...]) for hn in hns_q]
    ckvs = [_dot(hn, w_dkv_ref[...]) for hn in hns_kv]
    cqns = [_rms_norm(cq, qg_ref[...]).astype(BF16) for cq in cqs]
    qs = [_dot(cqn, w_uq_ref[...]) for cqn in cqns]
    if with_kv:
        c_lats = [_rms_norm(ckv[:, :kv_lora], lat_g_ref[...]).astype(BF16)
                  for ckv in ckvs]
        k_nopes = [_dot(c_lat, w_uk_ref[...]) for c_lat in c_lats]
        vts = [lax.dot_general(w_uvt_ref[...], c_lat, (((1,), (1,)), ((), ())),
                               preferred_element_type=F32) for c_lat in c_lats]
    for i, rs in enumerate(subs):
        _store_q(q_ref, rs, qs[i], css[i], q_scale)
    if with_kv:
        lane = lax.broadcasted_iota(jnp.int32, (sub_rows, LANES), 1)
        for i, rs in enumerate(subs):
            pe = _rope_block(ckvs[i][:, kv_lora:], css[i])
            k_pe_ref[rs, :] = jnp.where(lane < QK_ROPE_DIM, pe, 0.0).astype(BF16)
            k_nope_ref[rs, :] = k_nopes[i].astype(BF16)
            ones = jnp.ones((VT_ROWS - V_HEAD_DIM, sub_rows), BF16)
            for hd in range(N_HEADS):
                vt_ref[hd * VT_ROWS:hd * VT_ROWS + V_HEAD_DIM, rs] = (
                    vts[i][hd * V_HEAD_DIM:(hd + 1) * V_HEAD_DIM, :].astype(BF16))
                vt_ref[hd * VT_ROWS + V_HEAD_DIM:(hd + 1) * VT_ROWS, rs] = ones


def _kv_q(h, cs, q_weights, kv_weights, *, tm, q_scale):
    t, d = h.shape
    with_kv = kv_weights is not None
    args = [h, cs, *q_weights]
    specs = [_row_tile(tm, d), _rope_tile(tm)] + [_resident(w.shape) for w in q_weights]
    out_specs = [_row_tile(tm, N_HEADS * HEAD_PAD)]
    out_shape = [jax.ShapeDtypeStruct((t, N_HEADS * HEAD_PAD), BF16)]
    kv_lora = 0
    if with_kv:
        kv_lora = kv_weights[2].shape[1]
        args += list(kv_weights)
        specs += [_resident(w.shape) for w in kv_weights]
        out_specs += [_row_tile(tm, N_HEADS * QK_NOPE_DIM), _row_tile(tm, LANES),
                      pl.BlockSpec((N_HEADS * VT_ROWS, tm), lambda i: (0, i))]
        out_shape += [jax.ShapeDtypeStruct((t, N_HEADS * QK_NOPE_DIM), BF16),
                      jax.ShapeDtypeStruct((t, LANES), BF16),
                      jax.ShapeDtypeStruct((N_HEADS * VT_ROWS, t), BF16)]
    kernel = functools.partial(_kv_q_kernel, kv_lora=kv_lora, q_scale=q_scale,
                               with_kv=with_kv, sub_rows=FFN_SUB_ROWS)
    out = pl.pallas_call(
        kernel,
        grid=(t // tm,),
        in_specs=specs,
        out_specs=out_specs,
        out_shape=out_shape,
        compiler_params=_params(("parallel",)),
        name="kv_q" if with_kv else "q_proj",
    )(*args)
    return out if with_kv else out[0]


def _attention_kernel(q_ref, k_nope_ref, k_pe_ref, vt_ref, o_ref, k_sc, vt_sc,
                      *, heads, tq):
    seq = q_ref.shape[0]
    key = lax.broadcasted_iota(jnp.int32, (tq, tq), 0)
    qry = lax.broadcasted_iota(jnp.int32, (tq, tq), 1)
    causal = key <= qry
    neg = jnp.finfo(F32).min

    def scores(hd, off):
        qc = slice(hd * HEAD_PAD, (hd + 1) * HEAD_PAD)
        k_sc[hd, off:off + tq, 0:QK_NOPE_DIM] = (
            k_nope_ref[off:off + tq, hd * QK_NOPE_DIM:(hd + 1) * QK_NOPE_DIM])
        k_sc[hd, off:off + tq, QK_NOPE_DIM:HEAD_PAD] = k_pe_ref[off:off + tq, :]
        q = q_ref[off:off + tq, qc]
        k = k_sc[hd, 0:off + tq, :]
        return lax.dot_general(k, q, (((1,), (1,)), ((), ())),
                               preferred_element_type=F32)

    def finish(hd, off, st):
        vc = slice(hd * V_HEAD_DIM, (hd + 1) * V_HEAD_DIM)
        s_diag = jnp.where(causal, st[off:, :], neg)
        m = jnp.max(s_diag, axis=0, keepdims=True)
        if off:
            s_past = st[:off, :]
            m = jnp.maximum(m, jnp.max(s_past, axis=0, keepdims=True))
            pt = jnp.concatenate([jnp.exp2(s_past - m).astype(BF16),
                                  jnp.exp2(s_diag - m).astype(BF16)], axis=0)
        else:
            pt = jnp.exp2(s_diag - m).astype(BF16)
        vr = slice(hd * VT_ROWS, (hd + 1) * VT_ROWS)
        vt_sc[vr, off:off + tq] = vt_ref[vr, off:off + tq]
        out_t = _dot(vt_sc[vr, 0:off + tq], pt)
        inv_denom = 1.0 / out_t[V_HEAD_DIM:V_HEAD_DIM + 1, :]
        o_ref[off:off + tq, vc] = (out_t[:V_HEAD_DIM, :] * inv_denom).T.astype(BF16)

    tiles = [(hd, off) for off in range(0, seq, tq) for hd in range(heads)]
    pending = [scores(*tile) for tile in tiles[:ATTN_LOOKAHEAD]]
    for n, tile in enumerate(tiles):
        if n + ATTN_LOOKAHEAD < len(tiles):
            pending.append(scores(*tiles[n + ATTN_LOOKAHEAD]))
        finish(*tile, pending.pop(0))


def _attention(q, k_nope, k_pe, vt, *, batch, seq_len):
    hp = ATTN_HEADS_PER_STEP
    kernel = functools.partial(_attention_kernel, heads=hp, tq=ATTN_Q_TILE)
    q_spec = pl.BlockSpec((seq_len, hp * HEAD_PAD), lambda b, g: (b, g))
    k_nope_spec = pl.BlockSpec((seq_len, hp * QK_NOPE_DIM), lambda b, g: (b, g))
    k_pe_spec = pl.BlockSpec((seq_len, LANES), lambda b, g: (b, 0))
    vt_spec = pl.BlockSpec((hp * VT_ROWS, seq_len), lambda b, g: (g, b))
    out_spec = pl.BlockSpec((seq_len, hp * V_HEAD_DIM), lambda b, g: (b, g))
    return pl.pallas_call(
        kernel,
        grid=(batch, N_HEADS // hp),
        in_specs=[q_spec, k_nope_spec, k_pe_spec, vt_spec],
        out_specs=out_spec,
        out_shape=jax.ShapeDtypeStruct((batch * seq_len, N_HEADS * V_HEAD_DIM), BF16),
        scratch_shapes=[pltpu.VMEM((hp, seq_len, HEAD_PAD), BF16),
                        pltpu.VMEM((hp * VT_ROWS, seq_len), BF16)],
        compiler_params=_params(("parallel", "parallel")),
        name="attention",
    )(q, k_nope, k_pe, vt)


def _rope_columns(w_rope):
    half = QK_ROPE_DIM // 2
    x1, x2 = w_rope[..., :half], w_rope[..., half:]
    return jnp.concatenate([x1, x2, -x2, x1], axis=-1)


def _group_q_weight(w_uq):
    q_lora = w_uq.shape[0]
    w = w_uq.reshape(q_lora, N_HEADS, QK_NOPE_DIM + QK_ROPE_DIM)
    return jnp.concatenate(
        [w[..., :QK_NOPE_DIM].reshape(q_lora, N_HEADS * QK_NOPE_DIM),
         w[..., QK_NOPE_DIM:].reshape(q_lora, N_HEADS * QK_ROPE_DIM)], axis=-1)


def kernel(x, positions, conv_norm_g, conv_w_in, conv_w, conv_w_out, conv_ffn_norm_g, conv_ffn_w13, conv_ffn_w2, kv_norm_g, w_dkv, kv_latent_norm_g, w_ukv, mla_norm_g, mla_w_dq, mla_q_norm_g, mla_w_uq, mla_w_o, mla_ffn_norm_g, mla_ffn_w13, mla_ffn_w2, final_norm_g):
    batch, seq_len, d = x.shape
    t = batch * seq_len
    tm = TOKEN_TILE
    n_conv, n_mla = conv_w_in.shape[0], mla_w_dq.shape[0]
    kv_lora = kv_latent_norm_g.shape[0]
    assert seq_len % tm == 0 and seq_len % ATTN_Q_TILE == 0

    inv_freq = ROPE_THETA ** (-jnp.arange(0, QK_ROPE_DIM, 2, dtype=F32) / QK_ROPE_DIM)
    ang = positions.astype(F32).reshape(1, t) * inv_freq.reshape(-1, 1)
    cos, sin = jnp.cos(ang), jnp.sin(ang)
    cs = jnp.concatenate([cos, cos, sin, sin], axis=0)

    row = lambda g: g.reshape(1, -1)
    h = x.reshape(t, d)

    for i in range(n_conv):
        h = _conv_mixer(h, row(conv_norm_g[i]), conv_w_in, conv_w[i], conv_w_out, i,
                        seq_len=seq_len, tm=tm)
        h = _ffn(h, row(conv_ffn_norm_g[i]), conv_ffn_w13, conv_ffn_w2, i, tm=tm)

    w_dkv_p = jnp.concatenate([w_dkv[:, :kv_lora], _rope_columns(w_dkv[:, kv_lora:])], axis=-1)
    w_ukv_h = w_ukv.reshape(kv_lora, N_HEADS, QK_NOPE_DIM + V_HEAD_DIM)
    w_uk = w_ukv_h[..., :QK_NOPE_DIM].reshape(kv_lora, N_HEADS * QK_NOPE_DIM)
    w_uvt = w_ukv_h[..., QK_NOPE_DIM:].reshape(kv_lora, N_HEADS * V_HEAD_DIM).T
    kv_weights = (row(kv_norm_g), w_dkv_p.astype(BF16), row(kv_latent_norm_g),
                  w_uk.astype(BF16), w_uvt.astype(BF16))

    q_scale = float((QK_NOPE_DIM + QK_ROPE_DIM) ** -0.5 * 1.4426950408889634)
    for j in range(n_mla):
        q_weights = (row(mla_norm_g[j]), mla_w_dq[j].astype(BF16),
                     row(mla_q_norm_g[j]), _group_q_weight(mla_w_uq[j]).astype(BF16))
        if j == 0:
            q, k_nope, k_pe, vt = _kv_q(h, cs, q_weights, kv_weights, tm=tm,
                                        q_scale=q_scale)
        else:
            q = _kv_q(h, cs, q_weights, None, tm=tm, q_scale=q_scale)
        attn = _attention(q, k_nope, k_pe, vt, batch=batch, seq_len=seq_len)
        last = j == n_mla - 1
        h = _ffn(h, row(mla_ffn_norm_g[j]), mla_ffn_w13, mla_ffn_w2, j, tm=tm,
                 attn=attn, w_o=mla_w_o,
                 final_g=row(final_norm_g) if last else None)
    return h.reshape(batch, seq_len, d)
```

```python
import functools

import jax
import jax.numpy as jnp
from jax import lax
from jax.experimental import pallas as pl
from jax.experimental.pallas import tpu as pltpu

N_HEADS = 8
QK_NOPE_DIM = 128
QK_ROPE_DIM = 64
V_HEAD_DIM = 128
CONV_WIDTH = 3
ROPE_THETA = 10000.0
RMS_EPS = 1e-6

LANES = 128
SUBLANES = 8
HEAD_PAD = 2 * LANES
BF16_ROWS = 16
VT_ROWS = V_HEAD_DIM + BF16_ROWS
VMEM_LIMIT_BYTES = 58 * 1024 * 1024

TOKEN_TILE = 1024
ATTN_Q_TILE = 256
ATTN_HEADS_PER_STEP = 4
ATTN_LOOKAHEAD = 3
FFN_CHUNK = 1024
FFN_SUB_ROWS = 256
WEIGHT_CAST_STEPS = 16

BF16 = jnp.bfloat16
F32 = jnp.float32


def _dot(a, b):
    return jnp.dot(a, b, preferred_element_type=F32)


def _rms_norm(x, g):
    ms = jnp.mean(x * x, axis=-1, keepdims=True)
    return x * lax.rsqrt(ms + RMS_EPS) * g


def _resident(shape):
    return pl.BlockSpec(shape, lambda *_: (0,) * len(shape),
                        pipeline_mode=pl.Buffered(1))


def _row_tile(tm, width):
    return pl.BlockSpec((tm, width), lambda i: (i, 0))


def _rope_tile(tm):
    return pl.BlockSpec((LANES, tm), lambda i: (0, i))


def _params(semantics):
    return pltpu.CompilerParams(dimension_semantics=semantics,
                                vmem_limit_bytes=VMEM_LIMIT_BYTES)


def _weight_chunk(w_stacked, layer):
    _, rows, cols = w_stacked.shape
    last = WEIGHT_CAST_STEPS - 1
    return pl.BlockSpec((None, rows // WEIGHT_CAST_STEPS, cols),
                        lambda i: (layer, jnp.minimum(i, last), 0))


def _tile_after_cast(tm, width):
    return pl.BlockSpec((tm, width),
                        lambda i: (jnp.maximum(i - WEIGHT_CAST_STEPS, 0), 0))


def _cast_chunk(step, src_ref, dst_ref):
    rows = src_ref.shape[0]
    start = pl.multiple_of(step * rows, rows)
    dst_ref[pl.ds(start, rows), :] = src_ref[...].astype(BF16)


def _bf16_scratch(w_stacked):
    return pltpu.VMEM(w_stacked.shape[1:], BF16)


def _conv_mixer_kernel(h_ref, g_ref, w_in_f32_ref, cw_ref, w_out_f32_ref, o_ref,
                       w_in_ref, w_out_ref, carry_ref, *, tiles_per_seq, sub_rows):
    d = h_ref.shape[1]
    tm = h_ref.shape[0]
    step = pl.program_id(0)

    @pl.when(step < WEIGHT_CAST_STEPS)
    def _():
        _cast_chunk(step, w_in_f32_ref, w_in_ref)
        _cast_chunk(step, w_out_f32_ref, w_out_ref)

    @pl.when(step >= WEIGHT_CAST_STEPS)
    def _():
        @pl.when((step - WEIGHT_CAST_STEPS) % tiles_per_seq == 0)
        def _():
            carry_ref[...] = jnp.zeros_like(carry_ref)

        subs = [slice(r, r + sub_rows) for r in range(0, tm, sub_rows)]
        hs = [h_ref[rs, :] for rs in subs]
        hns = [_rms_norm(h, g_ref[...]).astype(BF16) for h in hs]
        gates_c = [_dot(hn, w_in_ref[:, d:2 * d]) for hn in hns]
        vals = [_dot(hn, w_in_ref[:, 2 * d:3 * d]) for hn in hns]
        gates_b = [_dot(hn, w_in_ref[:, 0:d]) for hn in hns]

        row = lax.broadcasted_iota(jnp.int32, (sub_rows, d), 0)
        cw = cw_ref[...]
        tail = carry_ref[...]
        for rs, h, gate_b, gate_c, val in zip(subs, hs, gates_b, gates_c, vals):
            z = gate_c * val
            prev1 = tail[SUBLANES - 1:SUBLANES, :]
            prev2 = tail[SUBLANES - 2:SUBLANES - 1, :]
            z1 = jnp.where(row == 0, prev1, pltpu.roll(z, 1, 0))
            z2 = jnp.where(row == 0, prev2,
                           jnp.where(row == 1, prev1, pltpu.roll(z, 2, 0)))
            conv = cw[0:1, :] * z2 + cw[1:2, :] * z1 + cw[2:3, :] * z
            tail = z[sub_rows - SUBLANES:sub_rows, :]
            y = (gate_b * conv).astype(BF16)
            o_ref[rs, :] = h + _dot(y, w_out_ref[...])
        carry_ref[...] = tail


def _conv_mixer(h, norm_g, w_in, conv_w, w_out, layer, *, seq_len, tm):
    t, d = h.shape
    kernel = functools.partial(_conv_mixer_kernel, tiles_per_seq=seq_len // tm,
                               sub_rows=FFN_SUB_ROWS)
    return pl.pallas_call(
        kernel,
        grid=(WEIGHT_CAST_STEPS + t // tm,),
        in_specs=[_tile_after_cast(tm, d), _resident((1, d)), _weight_chunk(w_in, layer),
                  _resident(conv_w.shape), _weight_chunk(w_out, layer)],
        out_specs=_tile_after_cast(tm, d),
        out_shape=jax.ShapeDtypeStruct((t, d), F32),
        scratch_shapes=[_bf16_scratch(w_in), _bf16_scratch(w_out),
                        pltpu.VMEM((SUBLANES, d), F32)],
        compiler_params=_params(("arbitrary",)),
        name="conv_mixer",
    )(h, norm_g, w_in, conv_w, w_out)


def _ffn_kernel(*refs, has_attn, has_final, chunks, sub_rows):
    refs = list(refs)
    h_ref = refs.pop(0)
    if has_attn:
        a_ref, wo_f32_ref = refs.pop(0), refs.pop(0)
    g_ref, w13_f32_ref, w2_f32_ref = refs.pop(0), refs.pop(0), refs.pop(0)
    if has_final:
        fg_ref = refs.pop(0)
    o_ref = refs.pop(0)
    if has_attn:
        wo_ref = refs.pop(0)
    w13_ref, w2_ref = refs
    d_ff = w2_ref.shape[0]
    step = pl.program_id(0)

    @pl.when(step < WEIGHT_CAST_STEPS)
    def _():
        _cast_chunk(step, w13_f32_ref, w13_ref)
        _cast_chunk(step, w2_f32_ref, w2_ref)
        if has_attn:
            _cast_chunk(step, wo_f32_ref, wo_ref)

    @pl.when(step >= WEIGHT_CAST_STEPS)
    def _():
        subs = [slice(r, r + sub_rows) for r in range(0, h_ref.shape[0], sub_rows)]
        accs, hns = [], []
        for rs in subs:
            h = h_ref[rs, :]
            if has_attn:
                h = h + _dot(a_ref[rs, :], wo_ref[...])
            accs.append(h)
            hns.append(_rms_norm(h, g_ref[...]).astype(BF16))
        for c0, c1 in chunks:
            gates = [_dot(hn, w13_ref[:, c0:c1]) for hn in hns]
            ups = [_dot(hn, w13_ref[:, d_ff + c0:d_ff + c1]) for hn in hns]
            for i, (gate, up) in enumerate(zip(gates, ups)):
                act = gate * (1.0 / (1.0 + jnp.exp(-gate))) * up
                accs[i] = accs[i] + _dot(act.astype(BF16), w2_ref[c0:c1, :])
        for rs, acc in zip(subs, accs):
            if has_final:
                acc = _rms_norm(acc, fg_ref[...])
            o_ref[rs, :] = acc


def _ffn(h, norm_g, w13, w2, layer, *, tm, attn=None, w_o=None, final_g=None):
    t, d = h.shape
    d_ff = w2.shape[1]
    chunks = tuple((c, min(c + FFN_CHUNK, d_ff)) for c in range(0, d_ff, FFN_CHUNK))
    has_attn, has_final = attn is not None, final_g is not None
    args, specs, scratch = [h], [_tile_after_cast(tm, d)], []
    if has_attn:
        args += [attn, w_o]
        specs += [_tile_after_cast(tm, attn.shape[1]), _weight_chunk(w_o, layer)]
        scratch.append(_bf16_scratch(w_o))
    args += [norm_g, w13, w2]
    specs += [_resident((1, d)), _weight_chunk(w13, layer), _weight_chunk(w2, layer)]
    scratch += [_bf16_scratch(w13), _bf16_scratch(w2)]
    if has_final:
        args.append(final_g)
        specs.append(_resident((1, d)))
    kernel = functools.partial(_ffn_kernel, has_attn=has_attn, has_final=has_final,
                               chunks=chunks, sub_rows=FFN_SUB_ROWS)
    return pl.pallas_call(
        kernel,
        grid=(WEIGHT_CAST_STEPS + t // tm,),
        in_specs=specs,
        out_specs=_tile_after_cast(tm, d),
        out_shape=jax.ShapeDtypeStruct((t, d), F32),
        scratch_shapes=scratch,
        compiler_params=_params(("arbitrary",)),
        name="ffn",
    )(*args)


def _rope_block(x, cs):
    t = x * cs
    return t + pltpu.roll(t, LANES // 2, 1)


def _store_q(q_ref, rs, q, cs, q_scale):
    lane = lax.broadcasted_iota(jnp.int32, cs.shape, 1)
    half = LANES // 2
    cs_swapped = pltpu.roll(cs, half, 1)
    cos4 = jnp.where(lane < half, cs, cs_swapped)
    first_half = (lane & (half - 1)) < half // 2
    sin4 = (jnp.where(lane < half, cs_swapped, cs)
            * jnp.where(first_half, -1.0, 1.0))
    nope_width = N_HEADS * QK_NOPE_DIM
    pe_cols = []
    for c in range(N_HEADS * QK_ROPE_DIM // LANES):
        x = q[:, nope_width + c * LANES:nope_width + (c + 1) * LANES]
        swapped = jnp.where(first_half, pltpu.roll(x, LANES - half // 2, 1),
                            pltpu.roll(x, half // 2, 1))
        pe_cols.append((x * cos4 + swapped * sin4) * q_scale)
    for hd in range(N_HEADS):
        lo = hd * HEAD_PAD
        q_ref[rs, lo:lo + QK_NOPE_DIM] = (
            q[:, hd * QK_NOPE_DIM:(hd + 1) * QK_NOPE_DIM] * q_scale).astype(BF16)
        col = pe_cols[hd // 2]
        if hd % 2:
            col = pltpu.roll(col, half, 1)
        q_ref[rs, lo + QK_NOPE_DIM:lo + HEAD_PAD] = col.astype(BF16)


def _kv_q_kernel(*refs, kv_lora, q_scale, with_kv, sub_rows):
    if with_kv:
        (h_ref, cs_ref, qn_g_ref, w_dq_ref, qg_ref, w_uq_ref, kv_g_ref, w_dkv_ref,
         lat_g_ref, w_uk_ref, w_uvt_ref, q_ref, k_ref, vt_ref) = refs
    else:
        h_ref, cs_ref, qn_g_ref, w_dq_ref, qg_ref, w_uq_ref, q_ref = refs
    subs = [slice(r, r + sub_rows) for r in range(0, h_ref.shape[0], sub_rows)]
    css = [cs_ref[:, rs].T for rs in subs]

    hns_q, hns_kv = [], []
    for rs in subs:
        h = h_ref[rs, :]
        unit = h * lax.rsqrt(jnp.mean(h * h, axis=-1, keepdims=True) + RMS_EPS)
        hns_q.append((unit * qn_g_ref[...]).astype(BF16))
        if with_kv:
            hns_kv.append((unit * kv_g_ref[...]).astype(BF16))

    cqs = [_dot(hn, w_dq_ref[...]) for hn in hns_q]
    ckvs = [_dot(hn, w_dkv_ref[...]) for hn in hns_kv]
    cqns = [_rms_norm(cq, qg_ref[...]).astype(BF16) for cq in cqs]
    qs = [_dot(cqn, w_uq_ref[...]) for cqn in cqns]
    if with_kv:
        c_lats = [_rms_norm(ckv[:, :kv_lora], lat_g_ref[...]).astype(BF16)
                  for ckv in ckvs]
        k_nopes = [_dot(c_lat, w_uk_ref[...]) for c_lat in c_lats]
        vts = [lax.dot_general(w_uvt_ref[...], c_lat, (((1,), (1,)), ((), ())),
                               preferred_element_type=F32) for c_lat in c_lats]
    for i, rs in enumerate(subs):
        _store_q(q_ref, rs, qs[i], css[i], q_scale)
    if with_kv:
        lane = lax.broadcasted_iota(jnp.int32, (sub_rows, LANES), 1)
        for i, rs in enumerate(subs):
            pe = _rope_block(ckvs[i][:, kv_lora:], css[i])
            k_pe = jnp.where(lane < QK_ROPE_DIM, pe, 0.0).astype(BF16)
            for hd in range(N_HEADS):
                k_ref[rs, hd * HEAD_PAD:hd * HEAD_PAD + QK_NOPE_DIM] = (
                    k_nopes[i][:, hd * QK_NOPE_DIM:(hd + 1) * QK_NOPE_DIM].astype(BF16))
                k_ref[rs, hd * HEAD_PAD + QK_NOPE_DIM:(hd + 1) * HEAD_PAD] = k_pe
            ones = jnp.ones((VT_ROWS - V_HEAD_DIM, sub_rows), BF16)
            for hd in range(N_HEADS):
                vt_ref[hd * VT_ROWS:hd * VT_ROWS + V_HEAD_DIM, rs] = (
                    vts[i][hd * V_HEAD_DIM:(hd + 1) * V_HEAD_DIM, :].astype(BF16))
                vt_ref[hd * VT_ROWS + V_HEAD_DIM:(hd + 1) * VT_ROWS, rs] = ones


def _kv_q(h, cs, q_weights, kv_weights, *, tm, q_scale):
    t, d = h.shape
    with_kv = kv_weights is not None
    args = [h, cs, *q_weights]
    specs = [_row_tile(tm, d), _rope_tile(tm)] + [_resident(w.shape) for w in q_weights]
    out_specs = [_row_tile(tm, N_HEADS * HEAD_PAD)]
    out_shape = [jax.ShapeDtypeStruct((t, N_HEADS * HEAD_PAD), BF16)]
    kv_lora = 0
    if with_kv:
        kv_lora = kv_weights[2].shape[1]
        args += list(kv_weights)
        specs += [_resident(w.shape) for w in kv_weights]
        out_specs += [_row_tile(tm, N_HEADS * HEAD_PAD),
                      pl.BlockSpec((N_HEADS * VT_ROWS, tm), lambda i: (0, i))]
        out_shape += [jax.ShapeDtypeStruct((t, N_HEADS * HEAD_PAD), BF16),
                      jax.ShapeDtypeStruct((N_HEADS * VT_ROWS, t), BF16)]
    kernel = functools.partial(_kv_q_kernel, kv_lora=kv_lora, q_scale=q_scale,
                               with_kv=with_kv, sub_rows=FFN_SUB_ROWS)
    out = pl.pallas_call(
        kernel,
        grid=(t // tm,),
        in_specs=specs,
        out_specs=out_specs,
        out_shape=out_shape,
        compiler_params=_params(("parallel",)),
        name="kv_q" if with_kv else "q_proj",
    )(*args)
    return out if with_kv else out[0]


def _attention_kernel(q_ref, k_ref, vt_ref, o_ref, k_sc, vt_sc, *, heads, tq):
    seq = q_ref.shape[0]
    key = lax.broadcasted_iota(jnp.int32, (tq, tq), 0)
    qry = lax.broadcasted_iota(jnp.int32, (tq, tq), 1)
    causal = key <= qry
    neg = jnp.finfo(F32).min

    def scores(hd, off):
        qc = slice(hd * HEAD_PAD, (hd + 1) * HEAD_PAD)
        k_sc[hd, off:off + tq, :] = k_ref[off:off + tq, qc]
        q = q_ref[off:off + tq, qc]
        k = k_sc[hd, 0:off + tq, :]
        return lax.dot_general(k, q, (((1,), (1,)), ((), ())),
                               preferred_element_type=F32)

    def finish(hd, off, st):
        vc = slice(hd * V_HEAD_DIM, (hd + 1) * V_HEAD_DIM)
        s_diag = jnp.where(causal, st[off:, :], neg)
        m = jnp.max(s_diag, axis=0, keepdims=True)
        if off:
            s_past = st[:off, :]
            m = jnp.maximum(m, jnp.max(s_past, axis=0, keepdims=True))
            pt = jnp.concatenate([jnp.exp2(s_past - m).astype(BF16),
                                  jnp.exp2(s_diag - m).astype(BF16)], axis=0)
        else:
            pt = jnp.exp2(s_diag - m).astype(BF16)
        vr = slice(hd * VT_ROWS, (hd + 1) * VT_ROWS)
        vt_sc[vr, off:off + tq] = vt_ref[vr, off:off + tq]
        out_t = _dot(vt_sc[vr, 0:off + tq], pt)
        inv_denom = 1.0 / out_t[V_HEAD_DIM:V_HEAD_DIM + 1, :]
        o_ref[off:off + tq, vc] = (out_t[:V_HEAD_DIM, :] * inv_denom).T.astype(BF16)

    tiles = [(hd, off) for off in range(0, seq, tq) for hd in range(heads)]
    pending = [scores(*tile) for tile in tiles[:ATTN_LOOKAHEAD]]
    for n, tile in enumerate(tiles):
        if n + ATTN_LOOKAHEAD < len(tiles):
            pending.append(scores(*tiles[n + ATTN_LOOKAHEAD]))
        finish(*tile, pending.pop(0))


def _attention(q, k, vt, *, batch, seq_len):
    hp = ATTN_HEADS_PER_STEP
    kernel = functools.partial(_attention_kernel, heads=hp, tq=ATTN_Q_TILE)
    qk_spec = pl.BlockSpec((seq_len, hp * HEAD_PAD), lambda b, g: (b, g))
    vt_spec = pl.BlockSpec((hp * VT_ROWS, seq_len), lambda b, g: (g, b))
    out_spec = pl.BlockSpec((seq_len, hp * V_HEAD_DIM), lambda b, g: (b, g))
    return pl.pallas_call(
        kernel,
        grid=(batch, N_HEADS // hp),
        in_specs=[qk_spec, qk_spec, vt_spec],
        out_specs=out_spec,
        out_shape=jax.ShapeDtypeStruct((batch * seq_len, N_HEADS * V_HEAD_DIM), BF16),
        scratch_shapes=[pltpu.VMEM((hp, seq_len, HEAD_PAD), BF16),
                        pltpu.VMEM((hp * VT_ROWS, seq_len), BF16)],
        compiler_params=_params(("parallel", "parallel")),
        name="attention",
    )(q, k, vt)


def _rope_columns(w_rope):
    half = QK_ROPE_DIM // 2
    x1, x2 = w_rope[..., :half], w_rope[..., half:]
    return jnp.concatenate([x1, x2, -x2, x1], axis=-1)


def _group_q_weight(w_uq):
    q_lora = w_uq.shape[0]
    w = w_uq.reshape(q_lora, N_HEADS, QK_NOPE_DIM + QK_ROPE_DIM)
    return jnp.concatenate(
        [w[..., :QK_NOPE_DIM].reshape(q_lora, N_HEADS * QK_NOPE_DIM),
         w[..., QK_NOPE_DIM:].reshape(q_lora, N_HEADS * QK_ROPE_DIM)], axis=-1)


def kernel(x, positions, conv_norm_g, conv_w_in, conv_w, conv_w_out, conv_ffn_norm_g, conv_ffn_w13, conv_ffn_w2, kv_norm_g, w_dkv, kv_latent_norm_g, w_ukv, mla_norm_g, mla_w_dq, mla_q_norm_g, mla_w_uq, mla_w_o, mla_ffn_norm_g, mla_ffn_w13, mla_ffn_w2, final_norm_g):
    batch, seq_len, d = x.shape
    t = batch * seq_len
    tm = TOKEN_TILE
    n_conv, n_mla = conv_w_in.shape[0], mla_w_dq.shape[0]
    kv_lora = kv_latent_norm_g.shape[0]
    assert seq_len % tm == 0 and seq_len % ATTN_Q_TILE == 0

    inv_freq = ROPE_THETA ** (-jnp.arange(0, QK_ROPE_DIM, 2, dtype=F32) / QK_ROPE_DIM)
    ang = positions.astype(F32).reshape(1, t) * inv_freq.reshape(-1, 1)
    cos, sin = jnp.cos(ang), jnp.sin(ang)
    cs = jnp.concatenate([cos, cos, sin, sin], axis=0)

    row = lambda g: g.reshape(1, -1)
    h = x.reshape(t, d)

    for i in range(n_conv):
        h = _conv_mixer(h, row(conv_norm_g[i]), conv_w_in, conv_w[i], conv_w_out, i,
                        seq_len=seq_len, tm=tm)
        h = _ffn(h, row(conv_ffn_norm_g[i]), conv_ffn_w13, conv_ffn_w2, i, tm=tm)

    w_dkv_p = jnp.concatenate([w_dkv[:, :kv_lora], _rope_columns(w_dkv[:, kv_lora:])], axis=-1)
    w_ukv_h = w_ukv.reshape(kv_lora, N_HEADS, QK_NOPE_DIM + V_HEAD_DIM)
    w_uk = w_ukv_h[..., :QK_NOPE_DIM].reshape(kv_lora, N_HEADS * QK_NOPE_DIM)
    w_uvt = w_ukv_h[..., QK_NOPE_DIM:].reshape(kv_lora, N_HEADS * V_HEAD_DIM).T
    kv_weights = (row(kv_norm_g), w_dkv_p.astype(BF16), row(kv_latent_norm_g),
                  w_uk.astype(BF16), w_uvt.astype(BF16))

    q_scale = float((QK_NOPE_DIM + QK_ROPE_DIM) ** -0.5 * 1.4426950408889634)
    for j in range(n_mla):
        q_weights = (row(mla_norm_g[j]), mla_w_dq[j].astype(BF16),
                     row(mla_q_norm_g[j]), _group_q_weight(mla_w_uq[j]).astype(BF16))
        if j == 0:
            q, k, vt = _kv_q(h, cs, q_weights, kv_weights, tm=tm, q_scale=q_scale)
        else:
            q = _kv_q(h, cs, q_weights, None, tm=tm, q_scale=q_scale)
        attn = _attention(q, k, vt, batch=batch, seq_len=seq_len)
        last = j == n_mla - 1
        h = _ffn(h, row(mla_ffn_norm_g[j]), mla_ffn_w13, mla_ffn_w2, j, tm=tm,
                 attn=attn, w_o=mla_w_o,
                 final_g=row(final_norm_g) if last else None)
    return h.reshape(batch, seq_len, d)
```

```python
import functools

import jax
import jax.numpy as jnp
from jax import lax
from jax.experimental import pallas as pl
from jax.experimental.pallas import tpu as pltpu

N_HEADS = 8
QK_NOPE_DIM = 128
QK_ROPE_DIM = 64
V_HEAD_DIM = 128
CONV_WIDTH = 3
ROPE_THETA = 10000.0
RMS_EPS = 1e-6

LANES = 128
SUBLANES = 8
HEAD_PAD = 2 * LANES
BF16_ROWS = 16
VT_ROWS = V_HEAD_DIM + BF16_ROWS
VMEM_LIMIT_BYTES = 58 * 1024 * 1024

TOKEN_TILE = 1024
ATTN_Q_TILE = 256
ATTN_HEADS_PER_STEP = 4
ATTN_LOOKAHEAD = 4
FFN_CHUNK = 1024
FFN_SUB_ROWS = 256
WEIGHT_CAST_STEPS = 16

BF16 = jnp.bfloat16
F32 = jnp.float32


def _dot(a, b):
    return jnp.dot(a, b, preferred_element_type=F32)


def _rms_norm(x, g):
    ms = jnp.mean(x * x, axis=-1, keepdims=True)
    return x * lax.rsqrt(ms + RMS_EPS) * g


def _resident(shape):
    return pl.BlockSpec(shape, lambda *_: (0,) * len(shape),
                        pipeline_mode=pl.Buffered(1))


def _row_tile(tm, width):
    return pl.BlockSpec((tm, width), lambda i: (i, 0))


def _rope_tile(tm):
    return pl.BlockSpec((LANES, tm), lambda i: (0, i))


def _params(semantics):
    return pltpu.CompilerParams(dimension_semantics=semantics,
                                vmem_limit_bytes=VMEM_LIMIT_BYTES)


def _weight_chunk(w_stacked, layer):
    _, rows, cols = w_stacked.shape
    last = WEIGHT_CAST_STEPS - 1
    return pl.BlockSpec((None, rows // WEIGHT_CAST_STEPS, cols),
                        lambda i: (layer, jnp.minimum(i, last), 0))


def _tile_after_cast(tm, width):
    return pl.BlockSpec((tm, width),
                        lambda i: (jnp.maximum(i - WEIGHT_CAST_STEPS, 0), 0))


def _cast_chunk(step, src_ref, dst_ref):
    rows = src_ref.shape[0]
    start = pl.multiple_of(step * rows, rows)
    dst_ref[pl.ds(start, rows), :] = src_ref[...].astype(BF16)


def _bf16_scratch(w_stacked):
    return pltpu.VMEM(w_stacked.shape[1:], BF16)


def _conv_mixer_kernel(h_ref, g_ref, w_in_f32_ref, cw_ref, w_out_f32_ref, o_ref,
                       w_in_ref, w_out_ref, carry_ref, *, tiles_per_seq, sub_rows):
    d = h_ref.shape[1]
    tm = h_ref.shape[0]
    step = pl.program_id(0)

    @pl.when(step < WEIGHT_CAST_STEPS)
    def _():
        _cast_chunk(step, w_in_f32_ref, w_in_ref)
        _cast_chunk(step, w_out_f32_ref, w_out_ref)

    @pl.when(step >= WEIGHT_CAST_STEPS)
    def _():
        @pl.when((step - WEIGHT_CAST_STEPS) % tiles_per_seq == 0)
        def _():
            carry_ref[...] = jnp.zeros_like(carry_ref)

        subs = [slice(r, r + sub_rows) for r in range(0, tm, sub_rows)]
        hs = [h_ref[rs, :] for rs in subs]
        hns = [_rms_norm(h, g_ref[...]).astype(BF16) for h in hs]
        gates_c = [_dot(hn, w_in_ref[:, d:2 * d]) for hn in hns]
        vals = [_dot(hn, w_in_ref[:, 2 * d:3 * d]) for hn in hns]
        gates_b = [_dot(hn, w_in_ref[:, 0:d]) for hn in hns]

        row = lax.broadcasted_iota(jnp.int32, (sub_rows, d), 0)
        cw = cw_ref[...]
        tail = carry_ref[...]
        for rs, h, gate_b, gate_c, val in zip(subs, hs, gates_b, gates_c, vals):
            z = gate_c * val
            prev1 = tail[SUBLANES - 1:SUBLANES, :]
            prev2 = tail[SUBLANES - 2:SUBLANES - 1, :]
            z1 = jnp.where(row == 0, prev1, pltpu.roll(z, 1, 0))
            z2 = jnp.where(row == 0, prev2,
                           jnp.where(row == 1, prev1, pltpu.roll(z, 2, 0)))
            conv = cw[0:1, :] * z2 + cw[1:2, :] * z1 + cw[2:3, :] * z
            tail = z[sub_rows - SUBLANES:sub_rows, :]
            y = (gate_b * conv).astype(BF16)
            o_ref[rs, :] = h + _dot(y, w_out_ref[...])
        carry_ref[...] = tail


def _conv_mixer(h, norm_g, w_in, conv_w, w_out, layer, *, seq_len, tm):
    t, d = h.shape
    kernel = functools.partial(_conv_mixer_kernel, tiles_per_seq=seq_len // tm,
                               sub_rows=FFN_SUB_ROWS)
    return pl.pallas_call(
        kernel,
        grid=(WEIGHT_CAST_STEPS + t // tm,),
        in_specs=[_tile_after_cast(tm, d), _resident((1, d)), _weight_chunk(w_in, layer),
                  _resident(conv_w.shape), _weight_chunk(w_out, layer)],
        out_specs=_tile_after_cast(tm, d),
        out_shape=jax.ShapeDtypeStruct((t, d), F32),
        scratch_shapes=[_bf16_scratch(w_in), _bf16_scratch(w_out),
                        pltpu.VMEM((SUBLANES, d), F32)],
        compiler_params=_params(("arbitrary",)),
        name="conv_mixer",
    )(h, norm_g, w_in, conv_w, w_out)


def _ffn_kernel(*refs, has_attn, has_final, chunks, sub_rows):
    refs = list(refs)
    h_ref = refs.pop(0)
    if has_attn:
        a_ref, wo_f32_ref = refs.pop(0), refs.pop(0)
    g_ref, w13_f32_ref, w2_f32_ref = refs.pop(0), refs.pop(0), refs.pop(0)
    if has_final:
        fg_ref = refs.pop(0)
    o_ref = refs.pop(0)
    if has_attn:
        wo_ref = refs.pop(0)
    w13_ref, w2_ref = refs
    d_ff = w2_ref.shape[0]
    step = pl.program_id(0)

    @pl.when(step < WEIGHT_CAST_STEPS)
    def _():
        _cast_chunk(step, w13_f32_ref, w13_ref)
        _cast_chunk(step, w2_f32_ref, w2_ref)
        if has_attn:
            _cast_chunk(step, wo_f32_ref, wo_ref)

    @pl.when(step >= WEIGHT_CAST_STEPS)
    def _():
        subs = [slice(r, r + sub_rows) for r in range(0, h_ref.shape[0], sub_rows)]
        accs, hns = [], []
        for rs in subs:
            h = h_ref[rs, :]
            if has_attn:
                h = h + _dot(a_ref[rs, :], wo_ref[...])
            accs.append(h)
            hns.append(_rms_norm(h, g_ref[...]).astype(BF16))
        for c0, c1 in chunks:
            gates = [_dot(hn, w13_ref[:, c0:c1]) for hn in hns]
            ups = [_dot(hn, w13_ref[:, d_ff + c0:d_ff + c1]) for hn in hns]
            for i, (gate, up) in enumerate(zip(gates, ups)):
                act = gate * (1.0 / (1.0 + jnp.exp(-gate))) * up
                accs[i] = accs[i] + _dot(act.astype(BF16), w2_ref[c0:c1, :])
        for rs, acc in zip(subs, accs):
            if has_final:
                acc = _rms_norm(acc, fg_ref[...])
            o_ref[rs, :] = acc


def _ffn(h, norm_g, w13, w2, layer, *, tm, attn=None, w_o=None, final_g=None):
    t, d = h.shape
    d_ff = w2.shape[1]
    chunks = tuple((c, min(c + FFN_CHUNK, d_ff)) for c in range(0, d_ff, FFN_CHUNK))
    has_attn, has_final = attn is not None, final_g is not None
    args, specs, scratch = [h], [_tile_after_cast(tm, d)], []
    if has_attn:
        args += [attn, w_o]
        specs += [_tile_after_cast(tm, attn.shape[1]), _weight_chunk(w_o, layer)]
        scratch.append(_bf16_scratch(w_o))
    args += [norm_g, w13, w2]
    specs += [_resident((1, d)), _weight_chunk(w13, layer), _weight_chunk(w2, layer)]
    scratch += [_bf16_scratch(w13), _bf16_scratch(w2)]
    if has_final:
        args.append(final_g)
        specs.append(_resident((1, d)))
    kernel = functools.partial(_ffn_kernel, has_attn=has_attn, has_final=has_final,
                               chunks=chunks, sub_rows=FFN_SUB_ROWS)
    return pl.pallas_call(
        kernel,
        grid=(WEIGHT_CAST_STEPS + t // tm,),
        in_specs=specs,
        out_specs=_tile_after_cast(tm, d),
        out_shape=jax.ShapeDtypeStruct((t, d), F32),
        scratch_shapes=scratch,
        compiler_params=_params(("arbitrary",)),
        name="ffn",
    )(*args)


def _rope_block(x, cs):
    t = x * cs
    return t + pltpu.roll(t, LANES // 2, 1)


def _store_q(q_ref, rs, q, cs, q_scale):
    lane = lax.broadcasted_iota(jnp.int32, cs.shape, 1)
    half = LANES // 2
    cs_swapped = pltpu.roll(cs, half, 1)
    cos4 = jnp.where(lane < half, cs, cs_swapped)
    first_half = (lane & (half - 1)) < half // 2
    sin4 = (jnp.where(lane < half, cs_swapped, cs)
            * jnp.where(first_half, -1.0, 1.0))
    nope_width = N_HEADS * QK_NOPE_DIM
    pe_cols = []
    for c in range(N_HEADS * QK_ROPE_DIM // LANES):
        x = q[:, nope_width + c * LANES:nope_width + (c + 1) * LANES]
        swapped = jnp.where(first_half, pltpu.roll(x, LANES - half // 2, 1),
                            pltpu.roll(x, half // 2, 1))
        pe_cols.append((x * cos4 + swapped * sin4) * q_scale)
    for hd in range(N_HEADS):
        lo = hd * HEAD_PAD
        q_ref[rs, lo:lo + QK_NOPE_DIM] = (
            q[:, hd * QK_NOPE_DIM:(hd + 1) * QK_NOPE_DIM] * q_scale).astype(BF16)
        col = pe_cols[hd // 2]
        if hd % 2:
            col = pltpu.roll(col, half, 1)
        q_ref[rs, lo + QK_NOPE_DIM:lo + HEAD_PAD] = col.astype(BF16)


def _kv_q_kernel(*refs, kv_lora, q_scale, with_kv, sub_rows):
    if with_kv:
        (h_ref, cs_ref, qn_g_ref, w_dq_ref, qg_ref, w_uq_ref, kv_g_ref, w_dkv_ref,
         lat_g_ref, w_uk_ref, w_uvt_ref, q_ref, k_ref, vt_ref) = refs
    else:
        h_ref, cs_ref, qn_g_ref, w_dq_ref, qg_ref, w_uq_ref, q_ref = refs
    subs = [slice(r, r + sub_rows) for r in range(0, h_ref.shape[0], sub_rows)]
    css = [cs_ref[:, rs].T for rs in subs]

    hns_q, hns_kv = [], []
    for rs in subs:
        h = h_ref[rs, :]
        unit = h * lax.rsqrt(jnp.mean(h * h, axis=-1, keepdims=True) + RMS_EPS)
        hns_q.append((unit * qn_g_ref[...]).astype(BF16))
        if with_kv:
            hns_kv.append((unit * kv_g_ref[...]).astype(BF16))

    cqs = [_dot(hn, w_dq_ref[...]) for hn in hns_q]
    ckvs = [_dot(hn, w_dkv_ref[...]) for hn in hns_kv]
    cqns = [_rms_norm(cq, qg_ref[...]).astype(BF16) for cq in cqs]
    qs = [_dot(cqn, w_uq_ref[...]) for cqn in cqns]
    if with_kv:
        c_lats = [_rms_norm(ckv[:, :kv_lora], lat_g_ref[...]).astype(BF16)
                  for ckv in ckvs]
        k_nopes = [_dot(c_lat, w_uk_ref[...]) for c_lat in c_lats]
        vts = [lax.dot_general(w_uvt_ref[...], c_lat, (((1,), (1,)), ((), ())),
                               preferred_element_type=F32) for c_lat in c_lats]
    for i, rs in enumerate(subs):
        _store_q(q_ref, rs, qs[i], css[i], q_scale)
    if with_kv:
        lane = lax.broadcasted_iota(jnp.int32, (sub_rows, LANES), 1)
        for i, rs in enumerate(subs):
            pe = _rope_block(ckvs[i][:, kv_lora:], css[i])
            k_pe = jnp.where(lane < QK_ROPE_DIM, pe, 0.0).astype(BF16)
            for hd in range(N_HEADS):
                k_ref[rs, hd * HEAD_PAD:hd * HEAD_PAD + QK_NOPE_DIM] = (
                    k_nopes[i][:, hd * QK_NOPE_DIM:(hd + 1) * QK_NOPE_DIM].astype(BF16))
                k_ref[rs, hd * HEAD_PAD + QK_NOPE_DIM:(hd + 1) * HEAD_PAD] = k_pe
            ones = jnp.ones((VT_ROWS - V_HEAD_DIM, sub_rows), BF16)
            for hd in range(N_HEADS):
                vt_ref[hd * VT_ROWS:hd * VT_ROWS + V_HEAD_DIM, rs] = (
                    vts[i][hd * V_HEAD_DIM:(hd + 1) * V_HEAD_DIM, :].astype(BF16))
                vt_ref[hd * VT_ROWS + V_HEAD_DIM:(hd + 1) * VT_ROWS, rs] = ones


def _kv_q(h, cs, q_weights, kv_weights, *, tm, q_scale):
    t, d = h.shape
    with_kv = kv_weights is not None
    args = [h, cs, *q_weights]
    specs = [_row_tile(tm, d), _rope_tile(tm)] + [_resident(w.shape) for w in q_weights]
    out_specs = [_row_tile(tm, N_HEADS * HEAD_PAD)]
    out_shape = [jax.ShapeDtypeStruct((t, N_HEADS * HEAD_PAD), BF16)]
    kv_lora = 0
    if with_kv:
        kv_lora = kv_weights[2].shape[1]
        args += list(kv_weights)
        specs += [_resident(w.shape) for w in kv_weights]
        out_specs += [_row_tile(tm, N_HEADS * HEAD_PAD),
                      pl.BlockSpec((N_HEADS * VT_ROWS, tm), lambda i: (0, i))]
        out_shape += [jax.ShapeDtypeStruct((t, N_HEADS * HEAD_PAD), BF16),
                      jax.ShapeDtypeStruct((N_HEADS * VT_ROWS, t), BF16)]
    kernel = functools.partial(_kv_q_kernel, kv_lora=kv_lora, q_scale=q_scale,
                               with_kv=with_kv, sub_rows=FFN_SUB_ROWS)
    out = pl.pallas_call(
        kernel,
        grid=(t // tm,),
        in_specs=specs,
        out_specs=out_specs,
        out_shape=out_shape,
        compiler_params=_params(("parallel",)),
        name="kv_q" if with_kv else "q_proj",
    )(*args)
    return out if with_kv else out[0]


def _attention_kernel(q_ref, k_ref, vt_ref, o_ref, k_sc, vt_sc, *, heads, tq):
    seq = q_ref.shape[0]
    key = lax.broadcasted_iota(jnp.int32, (tq, tq), 0)
    qry = lax.broadcasted_iota(jnp.int32, (tq, tq), 1)
    causal = key <= qry
    neg = jnp.finfo(F32).min

    def scores(hd, off):
        qc = slice(hd * HEAD_PAD, (hd + 1) * HEAD_PAD)
        k_sc[hd, off:off + tq, :] = k_ref[off:off + tq, qc]
        q = q_ref[off:off + tq, qc]
        k = k_sc[hd, 0:off + tq, :]
        return lax.dot_general(k, q, (((1,), (1,)), ((), ())),
                               preferred_element_type=F32)

    def finish(hd, off, st):
        vc = slice(hd * V_HEAD_DIM, (hd + 1) * V_HEAD_DIM)
        s_diag = jnp.where(causal, st[off:, :], neg)
        m = jnp.max(s_diag, axis=0, keepdims=True)
        if off:
            s_past = st[:off, :]
            m = jnp.maximum(m, jnp.max(s_past, axis=0, keepdims=True))
            pt = jnp.concatenate([jnp.exp2(s_past - m).astype(BF16),
                                  jnp.exp2(s_diag - m).astype(BF16)], axis=0)
        else:
            pt = jnp.exp2(s_diag - m).astype(BF16)
        vr = slice(hd * VT_ROWS, (hd + 1) * VT_ROWS)
        vt_sc[vr, off:off + tq] = vt_ref[vr, off:off + tq]
        out_t = _dot(vt_sc[vr, 0:off + tq], pt)
        inv_denom = 1.0 / out_t[V_HEAD_DIM:V_HEAD_DIM + 1, :]
        o_ref[off:off + tq, vc] = (out_t[:V_HEAD_DIM, :] * inv_denom).T.astype(BF16)

    tiles = [(hd, off) for off in range(0, seq, tq) for hd in range(heads)]
    pending = [scores(*tile) for tile in tiles[:ATTN_LOOKAHEAD]]
    for n, tile in enumerate(tiles):
        if n + ATTN_LOOKAHEAD < len(tiles):
            pending.append(scores(*tiles[n + ATTN_LOOKAHEAD]))
        finish(*tile, pending.pop(0))


def _attention(q, k, vt, *, batch, seq_len):
    hp = ATTN_HEADS_PER_STEP
    kernel = functools.partial(_attention_kernel, heads=hp, tq=ATTN_Q_TILE)
    qk_spec = pl.BlockSpec((seq_len, hp * HEAD_PAD), lambda b, g: (b, g))
    vt_spec = pl.BlockSpec((hp * VT_ROWS, seq_len), lambda b, g: (g, b))
    out_spec = pl.BlockSpec((seq_len, hp * V_HEAD_DIM), lambda b, g: (b, g))
    return pl.pallas_call(
        kernel,
        grid=(batch, N_HEADS // hp),
        in_specs=[qk_spec, qk_spec, vt_spec],
        out_specs=out_spec,
        out_shape=jax.ShapeDtypeStruct((batch * seq_len, N_HEADS * V_HEAD_DIM), BF16),
        scratch_shapes=[pltpu.VMEM((hp, seq_len, HEAD_PAD), BF16),
                        pltpu.VMEM((hp * VT_ROWS, seq_len), BF16)],
        compiler_params=_params(("parallel", "parallel")),
        name="attention",
    )(q, k, vt)


def _rope_columns(w_rope):
    half = QK_ROPE_DIM // 2
    x1, x2 = w_rope[..., :half], w_rope[..., half:]
    return jnp.concatenate([x1, x2, -x2, x1], axis=-1)


def _group_q_weight(w_uq):
    q_lora = w_uq.shape[0]
    w = w_uq.reshape(q_lora, N_HEADS, QK_NOPE_DIM + QK_ROPE_DIM)
    return jnp.concatenate(
        [w[..., :QK_NOPE_DIM].reshape(q_lora, N_HEADS * QK_NOPE_DIM),
         w[..., QK_NOPE_DIM:].reshape(q_lora, N_HEADS * QK_ROPE_DIM)], axis=-1)


def kernel(x, positions, conv_norm_g, conv_w_in, conv_w, conv_w_out, conv_ffn_norm_g, conv_ffn_w13, conv_ffn_w2, kv_norm_g, w_dkv, kv_latent_norm_g, w_ukv, mla_norm_g, mla_w_dq, mla_q_norm_g, mla_w_uq, mla_w_o, mla_ffn_norm_g, mla_ffn_w13, mla_ffn_w2, final_norm_g):
    batch, seq_len, d = x.shape
    t = batch * seq_len
    tm = TOKEN_TILE
    n_conv, n_mla = conv_w_in.shape[0], mla_w_dq.shape[0]
    kv_lora = kv_latent_norm_g.shape[0]
    assert seq_len % tm == 0 and seq_len % ATTN_Q_TILE == 0

    inv_freq = ROPE_THETA ** (-jnp.arange(0, QK_ROPE_DIM, 2, dtype=F32) / QK_ROPE_DIM)
    ang = positions.astype(F32).reshape(1, t) * inv_freq.reshape(-1, 1)
    cos, sin = jnp.cos(ang), jnp.sin(ang)
    cs = jnp.concatenate([cos, cos, sin, sin], axis=0)

    row = lambda g: g.reshape(1, -1)
    h = x.reshape(t, d)

    for i in range(n_conv):
        h = _conv_mixer(h, row(conv_norm_g[i]), conv_w_in, conv_w[i], conv_w_out, i,
                        seq_len=seq_len, tm=tm)
        h = _ffn(h, row(conv_ffn_norm_g[i]), conv_ffn_w13, conv_ffn_w2, i, tm=tm)

    w_dkv_p = jnp.concatenate([w_dkv[:, :kv_lora], _rope_columns(w_dkv[:, kv_lora:])], axis=-1)
    w_ukv_h = w_ukv.reshape(kv_lora, N_HEADS, QK_NOPE_DIM + V_HEAD_DIM)
    w_uk = w_ukv_h[..., :QK_NOPE_DIM].reshape(kv_lora, N_HEADS * QK_NOPE_DIM)
    w_uvt = w_ukv_h[..., QK_NOPE_DIM:].reshape(kv_lora, N_HEADS * V_HEAD_DIM).T
    kv_weights = (row(kv_norm_g), w_dkv_p.astype(BF16), row(kv_latent_norm_g),
                  w_uk.astype(BF16), w_uvt.astype(BF16))

    q_scale = float((QK_NOPE_DIM + QK_ROPE_DIM) ** -0.5 * 1.4426950408889634)
    for j in range(n_mla):
        q_weights = (row(mla_norm_g[j]), mla_w_dq[j].astype(BF16),
                     row(mla_q_norm_g[j]), _group_q_weight(mla_w_uq[j]).astype(BF16))
        if j == 0:
            q, k, vt = _kv_q(h, cs, q_weights, kv_weights, tm=tm, q_scale=q_scale)
        else:
            q = _kv_q(h, cs, q_weights, None, tm=tm, q_scale=q_scale)
        attn = _attention(q, k, vt, batch=batch, seq_len=seq_len)
        last = j == n_mla - 1
        h = _ffn(h, row(mla_ffn_norm_g[j]), mla_ffn_w13, mla_ffn_w2, j, tm=tm,
                 attn=attn, w_o=mla_w_o,
                 final_g=row(final_norm_g) if last else None)
    return h.reshape(batch, seq_len, d)
```
